```python
import jax
import jax.numpy as jnp
from jax import lax
import numpy as np

D_MODEL = 1024
BATCH = 2
SEQ = 16384
DEPTH = 2

GRID_W = 64
CTX_LEN = 256
HEAD_DIM = 64
MIX_W = D_MODEL // 4
N_BRANCH = 4
CONV_K = 3
WIN_HEADS = MIX_W // HEAD_DIM
WIN_KV_HEADS = WIN_HEADS // 2
WINDOW = 128
WIN_BLOCK = 128
LRU_BLOCKS = 4
LRU_CONV_K = 4
LRU_C = 8.0
NA_HEADS = MIX_W // HEAD_DIM
NA_ROWS = 8
NA_COLS = 16
D_FF = 256 * ((8 * D_MODEL // 3 + 255) // 256)
ROPE_BASE = 10000.0
EPS = 1e-6
NEG_INF = -1e30
N_MOD = 9
IN_SIZES = (MIX_W, MIX_W, MIX_W,
            WIN_HEADS * HEAD_DIM, WIN_KV_HEADS * HEAD_DIM, WIN_KV_HEADS * HEAD_DIM,
            MIX_W, MIX_W,
            NA_HEADS * HEAD_DIM, NA_HEADS * HEAD_DIM, NA_HEADS * HEAD_DIM,
            N_BRANCH * D_MODEL)
IN_W = sum(IN_SIZES)

kernel_name = 'hybrid_dit_prefix_gated_mixers'


def rms_norm(x, g):
    xf = x.astype(jnp.float32)
    y = xf * lax.rsqrt(jnp.mean(xf * xf, axis=-1, keepdims=True) + EPS)
    return (y * g.astype(jnp.float32)).astype(x.dtype)


def adaln(x, g, shift, scale):
    return rms_norm(x, g) * (1 + scale) + shift


def swiglu(h, w_gate, w_up, w_down):
    return (jax.nn.silu(h @ w_gate) * (h @ w_up)) @ w_down


def split_columns(u):
    points, acc = [], 0
    for size in IN_SIZES[:-1]:
        acc += size
        points.append(acc)
    return jnp.split(u, points, axis=-1)


def split_heads(t, n_heads):
    return t.reshape(t.shape[0], t.shape[1], n_heads, HEAD_DIM)


def dw_conv(x, w, pad_left):
    k, ch = w.shape
    return lax.conv_general_dilated(
        x, w[:, None, :].astype(x.dtype), window_strides=(1,),
        padding=[(pad_left, k - 1 - pad_left)],
        dimension_numbers=('NWC', 'WIO', 'NWC'), feature_group_count=ch)


def axial_rope(t, rows, cols):
    half = HEAD_DIM // 2
    nf = half // 2
    inv_freq = ROPE_BASE ** (-jnp.arange(nf, dtype=jnp.float32) / nf)

    def rotate(u, pos):
        ang = pos.astype(jnp.float32)[:, None] * inv_freq[None, :]
        cos = jnp.cos(ang)[None, :, None, :].astype(u.dtype)
        sin = jnp.sin(ang)[None, :, None, :].astype(u.dtype)
        u1, u2 = u[..., :nf], u[..., nf:]
        return jnp.concatenate([u1 * cos - u2 * sin, u1 * sin + u2 * cos], axis=-1)

    return jnp.concatenate([rotate(t[..., :half], rows), rotate(t[..., half:], cols)], axis=-1)


def context_attention(q, k, v, sink):
    b, l, hq, d = q.shape
    hkv = k.shape[2]
    g = hq // hkv
    qg = q.reshape(b, l, hkv, g, d)
    s = jnp.einsum('bqhgd,bkhd->bhgqk', qg, k).astype(jnp.float32) * (d ** -0.5)
    if sink is not None:
        s_sink = jnp.broadcast_to(sink.astype(jnp.float32).reshape(1, hkv, g, 1, 1), s.shape[:-1] + (1,))
        s = jnp.concatenate([s, s_sink], axis=-1)
    p = jax.nn.softmax(s, axis=-1)[..., :l].astype(v.dtype)
    o = jnp.einsum('bhgqk,bkhd->bqhgd', p, v)
    return o.reshape(b, l, hq * d)


def window_attention(q_rot, q_plain, k_rot, v, kc, vc, sink):
    b, n_tok, hq, d = q_rot.shape
    hkv = k_rot.shape[2]
    g = hq // hkv
    nb = n_tok // WIN_BLOCK
    scale = d ** -0.5
    qb = q_rot.reshape(b, nb, WIN_BLOCK, hkv, g, d)
    qcb = q_plain.reshape(b, nb, WIN_BLOCK, hkv, g, d)

    def band(t):
        tp = jnp.pad(t, ((0, 0), (WIN_BLOCK, WIN_BLOCK), (0, 0), (0, 0)))
        tp = tp.reshape(b, nb + 2, WIN_BLOCK, hkv, d)
        return jnp.concatenate([tp[:, :-2], tp[:, 1:-1], tp[:, 2:]], axis=2)

    kb, vb = band(k_rot), band(v)
    blk = jnp.arange(nb)[:, None, None]
    q_pos = blk * WIN_BLOCK + jnp.arange(WIN_BLOCK)[None, :, None]
    k_pos = (blk - 1) * WIN_BLOCK + jnp.arange(3 * WIN_BLOCK)[None, None, :]
    valid = (jnp.abs(k_pos - q_pos) <= WINDOW) & (k_pos >= 0) & (k_pos < n_tok)
    s_loc = jnp.einsum('bnqhgd,bnkhd->bnhgqk', qb, kb).astype(jnp.float32) * scale
    s_loc = jnp.where(valid[None, :, None, None], s_loc, NEG_INF)
    s_ctx = jnp.einsum('bnqhgd,blhd->bnhgql', qcb, kc).astype(jnp.float32) * scale
    s_sink = jnp.broadcast_to(sink.astype(jnp.float32).reshape(1, 1, hkv, g, 1, 1), s_loc.shape[:-1] + (1,))
    n_loc = 3 * WIN_BLOCK
    n_ctx = kc.shape[1]
    probs = jax.nn.softmax(jnp.concatenate([s_loc, s_ctx, s_sink], axis=-1), axis=-1).astype(v.dtype)
    o = (jnp.einsum('bnhgqk,bnkhd->bnqhgd', probs[..., :n_loc], vb)
         + jnp.einsum('bnhgql,blhd->bnqhgd', probs[..., n_loc:n_loc + n_ctx], vc))
    return o.reshape(b, n_tok, hq * d)


def neighbourhood_attention(q, k, v, kc, vc, rel_bias):
    b, n_tok, h, d = q.shape
    rows = n_tok // GRID_W
    kr = min(NA_ROWS, rows)
    ncb = GRID_W // NA_COLS
    cb = 2 * NA_COLS
    r = jnp.arange(rows)
    row_idx = jnp.clip(r - kr // 2, 0, rows - kr)[:, None] + jnp.arange(kr)[None, :]
    col_idx = (jnp.clip(jnp.arange(ncb) * NA_COLS - NA_COLS // 2, 0, GRID_W - cb)[:, None]
               + jnp.arange(cb)[None, :])
    q_col = jnp.arange(ncb)[:, None] * NA_COLS + jnp.arange(NA_COLS)[None, :]
    c_start = jnp.clip(q_col - NA_COLS // 2, 0, GRID_W - NA_COLS)[:, :, None]
    col_valid = (col_idx[:, None, :] >= c_start) & (col_idx[:, None, :] < c_start + NA_COLS)
    dr = row_idx - r[:, None] + NA_ROWS - 1
    dc = jnp.clip(col_idx[:, None, :] - q_col[:, :, None], 1 - NA_COLS, NA_COLS - 1) + NA_COLS - 1
    bias = rel_bias[:, dr[:, None, None, :, None], dc[None, :, :, None, :]]
    bias = jnp.transpose(bias, (1, 2, 0, 3, 4, 5)).astype(jnp.float32)
    qg = q.reshape(b, rows, ncb, NA_COLS, h, d)
    gi, gj = row_idx[:, None, :, None], col_idx[None, :, None, :]
    kg = k.reshape(b, rows, GRID_W, h, d)[:, gi, gj]
    vg = v.reshape(b, rows, GRID_W, h, d)[:, gi, gj]
    scale = d ** -0.5
    s_loc = jnp.einsum('brnqhd,brnkchd->brnhqkc', qg, kg).astype(jnp.float32) * scale + bias
    s_loc = jnp.where(col_valid[None, None, :, None, :, None, :], s_loc, NEG_INF)
    s_ctx = jnp.einsum('brnqhd,blhd->brnhql', qg, kc).astype(jnp.float32) * scale
    n_loc = kr * cb
    scores = jnp.concatenate([s_loc.reshape(s_ctx.shape[:-1] + (n_loc,)), s_ctx], axis=-1)
    probs = jax.nn.softmax(scores, axis=-1).astype(v.dtype)
    p_loc = probs[..., :n_loc].reshape(s_loc.shape)
    o = (jnp.einsum('brnhqkc,brnkchd->brnqhd', p_loc, vg)
         + jnp.einsum('brnhql,blhd->brnqhd', probs[..., n_loc:], vc))
    return o.reshape(b, n_tok, h * d)


def short_conv_mixer(u, gate_b, gate_c, w_conv):
    return gate_b * dw_conv(gate_c * u, w_conv, CONV_K // 2)


def rglru_coeffs(u, w_a, b_a, w_x, b_x, lam):
    b, t, w = u.shape
    ub = u.reshape(b, t, LRU_BLOCKS, w // LRU_BLOCKS)
    r = jax.nn.sigmoid(jnp.einsum('btnc,nce->btne', ub, w_a).reshape(b, t, w) + b_a).astype(jnp.float32)
    i = jax.nn.sigmoid(jnp.einsum('btnc,nce->btne', ub, w_x).reshape(b, t, w) + b_x)
    log_a = -LRU_C * r * jax.nn.softplus(-lam.astype(jnp.float32))
    return jnp.exp(log_a), jnp.sqrt(-jnp.expm1(2.0 * log_a)) * (i * u).astype(jnp.float32)


def linear_scan(a, b, h0, reverse):
    if h0 is not None:
        t0 = a.shape[1] - 1 if reverse else 0
        b = b.at[:, t0].add(a[:, t0] * h0)

    def combine(e1, e2):
        return e1[0] * e2[0], e2[0] * e1[1] + e2[1]

    _, h = lax.associative_scan(combine, (a, b), reverse=reverse, axis=1)
    return h


def rglru_mixer(xr, gr, xrc, grc, p, ctx_out):
    u = dw_conv(xr, p['lru_conv_w'], LRU_CONV_K // 2) + p['lru_conv_b']
    uc = dw_conv(xrc, p['lru_conv_w'], LRU_CONV_K // 2) + p['lru_conv_b']
    h_lat, h_ctx = [], []
    for direction, reverse in ((0, False), (1, True)):
        prm = (p['lru_w_a'][direction], p['lru_b_a'][direction], p['lru_w_x'][direction],
               p['lru_b_x'][direction], p['lru_lam'][direction])
        a_c, b_c = rglru_coeffs(uc, *prm)
        hc = linear_scan(a_c, b_c, None, reverse)
        h0 = hc[:, 0] if reverse else hc[:, -1]
        a_l, b_l = rglru_coeffs(u, *prm)
        h_lat.append(linear_scan(a_l, b_l, h0, reverse))
        h_ctx.append(hc)
    y = (h_lat[0] + h_lat[1]).astype(xr.dtype) * jax.nn.gelu(gr)
    if not ctx_out:
        return y, None
    return y, (h_ctx[0] + h_ctx[1]).astype(xrc.dtype) * jax.nn.gelu(grc)


def merge_branches(ys, gate_logits, w_branch, b_gate, w_out):
    b, t, _ = gate_logits.shape
    g = jax.nn.sigmoid(gate_logits + b_gate).reshape(b, t, N_BRANCH, D_MODEL)
    merged = g[:, :, 0] * (ys[0] @ w_branch[0])
    for n in range(1, N_BRANCH):
        merged = merged + g[:, :, n] * (ys[n] @ w_branch[n])
    return merged @ w_out


def token_mixer(h, hc, p, ctx_out):
    n_tok = h.shape[1]
    (ax, ab, ac, bq, bk, bv, cx, cg, dq, dk, dv, gl) = split_columns(h @ p['w_in'])
    (axc, abc, acc, bqc, bkc, bvc, cxc, cgc, dqc, dkc, dvc, glc) = split_columns(hc @ p['w_in'])
    gq_b, gk_b, gq_d, gk_d = p['qk_norm_g'][0], p['qk_norm_g'][1], p['qk_norm_g'][2], p['qk_norm_g'][3]
    pos = jnp.arange(n_tok)
    rows, cols = pos // GRID_W, pos % GRID_W
    qb = rms_norm(split_heads(bq, WIN_HEADS), gq_b)
    kb = rms_norm(split_heads(bk, WIN_KV_HEADS), gk_b)
    vb = split_heads(bv, WIN_KV_HEADS)
    kbc = rms_norm(split_heads(bkc, WIN_KV_HEADS), gk_b)
    vbc = split_heads(bvc, WIN_KV_HEADS)
    y_b = window_attention(axial_rope(qb, rows, cols), qb, axial_rope(kb, rows, cols), vb, kbc, vbc,
                           p['attn_sink'])
    qd = rms_norm(split_heads(dq, NA_HEADS), gq_d)
    kd = rms_norm(split_heads(dk, NA_HEADS), gk_d)
    vd = split_heads(dv, NA_HEADS)
    kdc = rms_norm(split_heads(dkc, NA_HEADS), gk_d)
    vdc = split_heads(dvc, NA_HEADS)
    y_d = neighbourhood_attention(qd, kd, vd, kdc, vdc, p['na_rel_bias'])
    y_a = short_conv_mixer(ax, ab, ac, p['conv_a_w'])
    y_c, y_c_ctx = rglru_mixer(cx, cg, cxc, cgc, p, ctx_out)
    out = merge_branches((y_a, y_b, y_c, y_d), gl, p['w_branch'], p['b_gate'], p['w_out'])
    if not ctx_out:
        return out, None
    qbc = rms_norm(split_heads(bqc, WIN_HEADS), gq_b)
    qdc = rms_norm(split_heads(dqc, NA_HEADS), gq_d)
    ys_ctx = (short_conv_mixer(axc, abc, acc, p['conv_a_w']),
              context_attention(qbc, kbc, vbc, p['attn_sink']),
              y_c_ctx,
              context_attention(qdc, kdc, vdc, None))
    return out, merge_branches(ys_ctx, glc, p['w_branch'], p['b_gate'], p['w_out'])


def layer(x, xc, mod_x, mod_c, p, ctx_out):
    mx = jnp.split(mod_x[:, None, :], N_MOD, axis=-1)
    mc = jnp.split(mod_c[:, None, :], N_MOD, axis=-1)
    g = p['norm_g']

    def half_ffn(t, m, i):
        j = 6 * i
        hh = adaln(t, g[2 * i], m[j], m[j + 1])
        return t + 0.5 * m[j + 2] * swiglu(hh, p['ffn_w_gate'][i], p['ffn_w_up'][i], p['ffn_w_down'][i])

    x = half_ffn(x, mx, 0)
    xc = half_ffn(xc, mc, 0)
    y, yc = token_mixer(adaln(x, g[1], mx[3], mx[4]), adaln(xc, g[1], mc[3], mc[4]), p, ctx_out)
    x = half_ffn(x + mx[5] * y, mx, 1)
    if not ctx_out:
        return x, None
    xc = half_ffn(xc + mc[5] * yc, mc, 1)
    return x, xc


def setup_inputs(seed: int = 0) -> dict:
    key = jax.random.key(seed)
    ks = jax.random.split(key, 26)
    f32 = jnp.float32

    def nrm(k, shape, scale):
        return scale * jax.random.normal(k, shape, f32)

    blk = MIX_W // LRU_BLOCKS
    u = jax.random.uniform(ks[20], (DEPTH, 2, MIX_W), f32, 0.9, 0.999)
    s = u ** (1.0 / LRU_C)
    lam = jnp.log(s) - jnp.log1p(-s)
    return {
        'x': nrm(ks[0], (BATCH, SEQ, D_MODEL), 1.0),
        'c': nrm(ks[1], (BATCH, D_MODEL), 1.0),
        'ctx': nrm(ks[2], (BATCH, CTX_LEN, D_MODEL), 1.0),
        'c_ctx': nrm(ks[3], (D_MODEL,), 1.0),
        'w_mod': nrm(ks[4], (DEPTH, D_MODEL, N_MOD * D_MODEL), 0.5 * D_MODEL ** -0.5),
        'b_mod': nrm(ks[5], (DEPTH, N_MOD * D_MODEL), 0.01),
        'norm_g': 1.0 + nrm(ks[6], (DEPTH, 3, D_MODEL), 0.02),
        'ffn_w_gate': nrm(ks[7], (DEPTH, 2, D_MODEL, D_FF), D_MODEL ** -0.5),
        'ffn_w_up': nrm(ks[8], (DEPTH, 2, D_MODEL, D_FF), D_MODEL ** -0.5),
        'ffn_w_down': nrm(ks[9], (DEPTH, 2, D_FF, D_MODEL), D_FF ** -0.5),
        'w_in': nrm(ks[10], (DEPTH, D_MODEL, IN_W), D_MODEL ** -0.5),
        'b_gate': nrm(ks[11], (DEPTH, N_BRANCH * D_MODEL), 0.1),
        'conv_a_w': nrm(ks[12], (DEPTH, CONV_K, MIX_W), CONV_K ** -0.5),
        'qk_norm_g': 1.0 + nrm(ks[13], (DEPTH, 4, HEAD_DIM), 0.02),
        'attn_sink': nrm(ks[14], (DEPTH, WIN_HEADS), 0.5),
        'lru_conv_w': nrm(ks[15], (DEPTH, LRU_CONV_K, MIX_W), LRU_CONV_K ** -0.5),
        'lru_conv_b': nrm(ks[16], (DEPTH, MIX_W), 0.01),
        'lru_w_a': nrm(ks[17], (DEPTH, 2, LRU_BLOCKS, blk, blk), blk ** -0.5),
        'lru_b_a': nrm(ks[18], (DEPTH, 2, MIX_W), 0.01),
        'lru_w_x': nrm(ks[19], (DEPTH, 2, LRU_BLOCKS, blk, blk), blk ** -0.5),
        'lru_b_x': nrm(ks[21], (DEPTH, 2, MIX_W), 0.01),
        'lru_lam': lam,
        'na_rel_bias': nrm(ks[22], (DEPTH, NA_HEADS, 2 * NA_ROWS - 1, 2 * NA_COLS - 1), 0.1),
        'w_branch': nrm(ks[23], (DEPTH, N_BRANCH, MIX_W, D_MODEL), MIX_W ** -0.5),
        'w_out': nrm(ks[24], (DEPTH, D_MODEL, D_MODEL), D_MODEL ** -0.5),
    }


def reference(x, c, ctx, c_ctx, w_mod, b_mod, norm_g, ffn_w_gate, ffn_w_up, ffn_w_down, w_in, b_gate,
              conv_a_w, qk_norm_g, attn_sink, lru_conv_w, lru_conv_b, lru_w_a, lru_b_a, lru_w_x, lru_b_x,
              lru_lam, na_rel_bias, w_branch, w_out):
    xc = ctx
    for l in range(DEPTH):
        ctx_out = l < DEPTH - 1
        mod_x = jax.nn.silu(c) @ w_mod[l] + b_mod[l]
        mod_c = (jax.nn.silu(c_ctx) @ w_mod[l] + b_mod[l])[None, :]
        p = dict(norm_g=norm_g[l], ffn_w_gate=ffn_w_gate[l], ffn_w_up=ffn_w_up[l], ffn_w_down=ffn_w_down[l],
                 w_in=w_in[l], b_gate=b_gate[l], conv_a_w=conv_a_w[l], qk_norm_g=qk_norm_g[l],
                 attn_sink=attn_sink[l], lru_conv_w=lru_conv_w[l], lru_conv_b=lru_conv_b[l],
                 lru_w_a=lru_w_a[l], lru_b_a=lru_b_a[l], lru_w_x=lru_w_x[l], lru_b_x=lru_b_x[l],
                 lru_lam=lru_lam[l], na_rel_bias=na_rel_bias[l], w_branch=w_branch[l], w_out=w_out[l])
        x, xc = layer(x, xc, mod_x, mod_c, p, ctx_out)
    return x
```

```python
import functools

import jax
import jax.numpy as jnp
from jax import lax
from jax.experimental import pallas as pl
from jax.experimental.pallas import tpu as pltpu

HEAD_DIM = 64
GRID_W = 64
WINDOW = 128
NA_ROWS = 8
NA_COLS = 16
N_BRANCH = 4
CONV_K = 3
LRU_CONV_K = 4
LRU_C = 8.0
ROPE_BASE = 10000.0
EPS = 1e-6
NEG_INF = -1e30
N_MOD = 9

V7X_VMEM_LIMIT_BYTES = 60 * 1024 * 1024
HALO = 8

F32 = jnp.float32
BF16 = jnp.bfloat16


def _params(n_grid):
    return pltpu.CompilerParams(
        dimension_semantics=("arbitrary",) * n_grid,
        vmem_limit_bytes=V7X_VMEM_LIMIT_BYTES)


def _dot(a, b):
    return jnp.dot(a, b, preferred_element_type=F32)


def _dot_nt(a, b):
    return lax.dot_general(a, b, (((1,), (1,)), ((), ())), preferred_element_type=F32)


def _sigmoid(x):
    return jax.nn.sigmoid(x)


def _gelu_tanh(x):
    return 0.5 * x * (1.0 + jnp.tanh(0.7978845608028654 * (x + 0.044715 * (x * x * x))))


def _adaln(x, g, shift, scale):
    ms = jnp.mean(x * x, axis=-1, keepdims=True)
    return (x * lax.rsqrt(ms + EPS) * g) * (1.0 + scale) + shift


def _full(shape):
    n = len(shape)
    return pl.BlockSpec(shape, lambda *_: (0,) * n)


def _mod_kernel(c_ref, w_ref, b_ref, o_ref):
    cs = c_ref[...]
    s = cs * _sigmoid(cs)
    o_ref[...] = jnp.dot(s, w_ref[...], preferred_element_type=F32,
                         precision=lax.Precision.HIGHEST) + b_ref[...]


def _modulation(cs, w_mod, b_mod):
    depth, d, n = w_mod.shape
    tn = n // 8
    return pl.pallas_call(
        _mod_kernel,
        grid=(depth, n // tn),
        in_specs=[pl.BlockSpec((8, d), lambda l, j: (0, 0)),
                  pl.BlockSpec((None, d, tn), lambda l, j: (l, 0, j)),
                  pl.BlockSpec((None, 1, tn), lambda l, j: (l, 0, j))],
        out_specs=pl.BlockSpec((None, 8, tn), lambda l, j: (l, 0, j)),
        out_shape=jax.ShapeDtypeStruct((depth, 8, n), F32),
        compiler_params=_params(2),
        name="modulation",
    )(cs, w_mod, b_mod.reshape(depth, 1, n))


def _ffn_kernel(x_ref, mod_ref, g_ref, wg_ref, wu_ref, wd_ref, o_ref, *, row0, chunks):
    x = x_ref[...]
    h = _adaln(x, g_ref[...], mod_ref[row0:row0 + 1, :], mod_ref[row0 + 1:row0 + 2, :]).astype(BF16)
    acc = None
    for (lo, hi) in chunks:
        a = _dot(h, wg_ref[:, lo:hi])
        u = _dot(h, wu_ref[:, lo:hi])
        act = (a * _sigmoid(a) * u).astype(BF16)
        part = _dot(act, wd_ref[lo:hi, :])
        acc = part if acc is None else acc + part
    o_ref[...] = x + (0.5 * mod_ref[row0 + 2:row0 + 3, :]) * acc


def _ffn(x, mod, g, wg, wu, wd, row0, tile):
    b, s, d = x.shape
    f = wg.shape[1]
    half = (f // 2 + 255) // 256 * 256
    chunks = ((0, half), (half, f)) if half < f else ((0, f),)
    return pl.pallas_call(
        functools.partial(_ffn_kernel, row0=row0, chunks=chunks),
        grid=(b, s // tile),
        in_specs=[pl.BlockSpec((None, tile, d), lambda bi, i: (bi, i, 0)),
                  pl.BlockSpec((None, N_MOD, d), lambda bi, i: (bi, 0, 0)),
                  _full((1, d)), _full((d, f)), _full((d, f)), _full((f, d))],
        out_specs=pl.BlockSpec((None, tile, d), lambda bi, i: (bi, i, 0)),
        out_shape=jax.ShapeDtypeStruct((b, s, d), F32),
        compiler_params=_params(2),
        name="ffn",
    )(x, mod, g, wg, wu, wd)


def _head_rms(t, m_ref, g):
    n = t.shape[1]
    sq = t * t
    hi = sq.astype(BF16)
    lo = (sq - hi.astype(F32)).astype(BF16)
    m = m_ref[0:n, 0:n]
    ms = _dot(hi, m) + _dot(lo, m)
    return t * lax.rsqrt(ms + EPS) * g


def _rope(t, cos, sin_signed):
    n = t.shape[1]
    lane = lax.broadcasted_iota(jnp.int32, t.shape, 1)
    partner = jnp.where((lane % 32) < 16, pltpu.roll(t, n - 16, 1), pltpu.roll(t, 16, 1))
    return t * cos + partner * sin_signed


def _inproj_kernel(x_ref, mod_ref, g_ref, w_ref, qkg_ref, m_ref, cos_ref, sin_ref,
                   pa_ref, ab_ref, qr_ref, qp_ref, kr_ref, vb_ref, cx_ref, cg_ref,
                   dq_ref, dk_ref, dv_ref, *, mw):
    x = x_ref[...]
    h = _adaln(x, g_ref[...], mod_ref[3:4, :], mod_ref[4:5, :]).astype(BF16)
    u = _dot(h, w_ref[...])
    kvw = mw // 2
    o = 0
    ax = u[:, o:o + mw]; o += mw
    ab = u[:, o:o + mw]; o += mw
    ac = u[:, o:o + mw]; o += mw
    bq = u[:, o:o + mw]; o += mw
    bk = u[:, o:o + kvw]; o += kvw
    bv = u[:, o:o + kvw]; o += kvw
    cx = u[:, o:o + mw]; o += mw
    cg = u[:, o:o + mw]; o += mw
    dq = u[:, o:o + mw]; o += mw
    dk = u[:, o:o + mw]; o += mw
    dv = u[:, o:o + mw]; o += mw
    pa_ref[...] = ac * ax
    ab_ref[...] = ab
    cx_ref[...] = cx
    cg_ref[...] = cg
    cos = cos_ref[...]
    sin = sin_ref[...]
    cos2 = jnp.concatenate([cos, cos], axis=1)
    sin2 = jnp.concatenate([sin, sin], axis=1)
    scale = HEAD_DIM ** -0.5
    q = _head_rms(bq, m_ref, qkg_ref[0:1, :]) * scale
    qp_ref[...] = q.astype(BF16)
    qr_ref[...] = _rope(q, cos2, sin2).astype(BF16)
    k = _head_rms(bk, m_ref, qkg_ref[1:2, 0:kvw])
    kr_ref[...] = _rope(k, cos, sin).astype(BF16)
    vb_ref[...] = bv.astype(BF16)
    dq_ref[...] = (_head_rms(dq, m_ref, qkg_ref[2:3, :]) * scale).astype(BF16)
    dk_ref[...] = _head_rms(dk, m_ref, qkg_ref[3:4, :]).astype(BF16)
    dv_ref[...] = dv.astype(BF16)


def _inproj(x, mod, g, w, qkg, mhead, cos, sin, tile):
    b, s, d = x.shape
    mw = qkg.shape[1]
    kvw = mw // 2
    nw = w.shape[1]
    tok = lambda width: pl.BlockSpec((None, tile, width), lambda bi, i: (bi, i, 0))
    widths = (mw, mw, mw, mw, kvw, kvw, mw, mw, mw, mw, mw)
    dtypes = (F32, F32, BF16, BF16, BF16, BF16, F32, F32, BF16, BF16, BF16)
    return pl.pallas_call(
        functools.partial(_inproj_kernel, mw=mw),
        grid=(b, s // tile),
        in_specs=[tok(d),
                  pl.BlockSpec((None, N_MOD, d), lambda bi, i: (bi, 0, 0)),
                  _full((1, d)), _full((d, nw)), _full((4, mw)), _full((mw, mw)),
                  pl.BlockSpec((tile, kvw), lambda bi, i: (i, 0)),
                  pl.BlockSpec((tile, kvw), lambda bi, i: (i, 0))],
        out_specs=[tok(wd) for wd in widths],
        out_shape=[jax.ShapeDtypeStruct((b, s, wd), dt) for wd, dt in zip(widths, dtypes)],
        compiler_params=_params(2),
        name="inproj",
    )(x, mod, g, w, qkg, mhead, cos, sin)


def _shift_rows(t, d, fill, reverse):
    n = t.shape[0]
    row = lax.broadcasted_iota(jnp.int32, t.shape, 0)
    if reverse:
        return jnp.where(row < n - d, pltpu.roll(t, n - d, 0), fill)
    return jnp.where(row >= d, pltpu.roll(t, d, 0), fill)


def _lru_tile(prev_ref, cur_ref, next_ref, has_prev, has_next, cw_ref, cb_ref,
              wa_ref, ba_ref, wx_ref, bx_ref, lam_ref, carry, direction):
    reverse = direction == 1
    cur = cur_ref[...]
    t = cur.shape[0]
    prev = jnp.where(has_prev, prev_ref[...], 0.0)
    nxt = jnp.where(has_next, next_ref[...], 0.0)
    ext = jnp.concatenate([prev, cur, nxt], axis=0)
    u = cb_ref[...] + cur * cw_ref[2:3, :]
    u = u + ext[HALO - 2:HALO - 2 + t] * cw_ref[0:1, :]
    u = u + ext[HALO - 1:HALO - 1 + t] * cw_ref[1:2, :]
    u = u + ext[HALO + 1:HALO + 1 + t] * cw_ref[3:4, :]
    ub = u.astype(BF16)
    r = _sigmoid(_dot(ub, wa_ref[direction]) + ba_ref[direction:direction + 1, :])
    ig = _sigmoid(_dot(ub, wx_ref[direction]) + bx_ref[direction:direction + 1, :])
    nl = -lam_ref[direction:direction + 1, :]
    softplus = jnp.maximum(nl, 0.0) + jnp.log1p(jnp.exp(-jnp.abs(nl)))
    log_a = (-LRU_C) * r * softplus
    a = jnp.exp(log_a)
    bb = jnp.sqrt(-jnp.tanh(log_a) * (a * a + 1.0)) * (ig * u)
    d = 1
    while d < t:
        a_s = _shift_rows(a, d, 1.0, reverse)
        b_s = _shift_rows(bb, d, 0.0, reverse)
        bb = a * b_s + bb
        a = a * a_s
        d *= 2
    return a * carry + bb


def _scan_kernel(fp_ref, fc_ref, fn_ref, rp_ref, rc_ref, rn_ref, h0_ref, cw_ref, cb_ref,
                 wa_ref, ba_ref, wx_ref, bx_ref, lam_ref, hf_ref, hb_ref, carry_ref):
    i = pl.program_id(1)
    nt = pl.num_programs(1)

    @pl.when(i == 0)
    def _():
        carry_ref[...] = h0_ref[...]

    t = fc_ref.shape[0]
    par = (cw_ref, cb_ref, wa_ref, ba_ref, wx_ref, bx_ref, lam_ref)
    hf = _lru_tile(fp_ref, fc_ref, fn_ref, i > 0, i < nt - 1, *par, carry_ref[0:1, :], 0)
    hf_ref[...] = hf
    carry_ref[0:1, :] = hf[t - 1:t, :]
    hb = _lru_tile(rp_ref, rc_ref, rn_ref, i < nt - 1, i > 0, *par, carry_ref[1:2, :], 1)
    hb_ref[...] = hb
    carry_ref[1:2, :] = hb[0:1, :]


def _halo_specs(tile, width, n_rows, tile_index):
    r = tile // HALO
    last = n_rows // HALO - 1
    prev = pl.BlockSpec((None, HALO, width),
                        lambda bi, i: (bi, jnp.maximum(tile_index(i) * r - 1, 0), 0))
    cur = pl.BlockSpec((None, tile, width), lambda bi, i: (bi, tile_index(i), 0))
    nxt = pl.BlockSpec((None, HALO, width),
                       lambda bi, i: (bi, jnp.minimum((tile_index(i) + 1) * r, last), 0))
    return [prev, cur, nxt]


def _scan(cx, h0, cw, cb, wa, ba, wx, bx, lam, tile):
    b, s, w = cx.shape
    nt = s // tile
    fwd = lambda i: i
    rev = lambda i: nt - 1 - i
    return pl.pallas_call(
        _scan_kernel,
        grid=(b, nt),
        in_specs=(_halo_specs(tile, w, s, fwd) + _halo_specs(tile, w, s, rev)
                  + [pl.BlockSpec((None, 2, w), lambda bi, i: (bi, 0, 0)),
                     _full((LRU_CONV_K, w)), _full((1, w)),
                     _full((2, w, w)), _full((2, w)), _full((2, w, w)), _full((2, w)), _full((2, w))]),
        out_specs=[pl.BlockSpec((None, tile, w), lambda bi, i: (bi, i, 0)),
                   pl.BlockSpec((None, tile, w), lambda bi, i: (bi, nt - 1 - i, 0))],
        out_shape=[jax.ShapeDtypeStruct((b, s, w), F32)] * 2,
        scratch_shapes=[pltpu.VMEM((2, w), F32)],
        compiler_params=_params(2),
        name="lru_scan",
    )(cx, cx, cx, cx, cx, cx, h0, cw, cb, wa, ba, wx, bx, lam)


def _softmax_pv(parts, sink):
    m = None
    for s, _ in parts:
        mx = jnp.max(s, axis=-1, keepdims=True)
        m = mx if m is None else jnp.maximum(m, mx)
    if sink is not None:
        m = jnp.maximum(m, sink)
    den = None
    out = None
    for s, v in parts:
        p = jnp.exp(s - m)
        sm = jnp.sum(p, axis=-1, keepdims=True)
        den = sm if den is None else den + sm
        pv = _dot(p.astype(BF16), v)
        out = pv if out is None else out + pv
    if sink is not None:
        den = den + jnp.exp(sink - m)
    return out / den


def _window_kernel(qr_ref, qp_ref, kp_ref, kc_ref, kn_ref, vp_ref, vc_ref, vn_ref,
                   kx_ref, vx_ref, sink_ref, o_ref, *, n_kv, group):
    i = pl.program_id(1)
    nt = pl.num_programs(1)
    tq = qr_ref.shape[0]
    nk = tq + 2 * WINDOW
    rows = group * tq
    r = lax.broadcasted_iota(jnp.int32, (rows, nk), 0) % tq
    kk = lax.broadcasted_iota(jnp.int32, (rows, nk), 1)
    rel = kk - WINDOW - r
    k_lo = jnp.where(i > 0, 0, WINDOW)
    k_hi = jnp.where(i < nt - 1, nk, tq + WINDOW)
    valid = (rel <= WINDOW) & (rel >= -WINDOW) & (kk >= k_lo) & (kk < k_hi)
    qr = qr_ref[...]
    qp = qp_ref[...]
    kcat = jnp.concatenate([kp_ref[...], kc_ref[...], kn_ref[...]], axis=0)
    vcat = jnp.concatenate([vp_ref[...], vc_ref[...], vn_ref[...]], axis=0)
    kx = kx_ref[...]
    vx = vx_ref[...]
    outs = []
    for hk in range(n_kv):
        ks = slice(hk * HEAD_DIM, (hk + 1) * HEAD_DIM)
        heads = range(hk * group, (hk + 1) * group)
        q_rot = jnp.concatenate([qr[:, h * HEAD_DIM:(h + 1) * HEAD_DIM] for h in heads], axis=0)
        q_pln = jnp.concatenate([qp[:, h * HEAD_DIM:(h + 1) * HEAD_DIM] for h in heads], axis=0)
        sink = jnp.concatenate(
            [jnp.broadcast_to(sink_ref[0:1, h:h + 1], (tq, 1)) for h in heads], axis=0)
        s_loc = jnp.where(valid, _dot_nt(q_rot, kcat[:, ks]), NEG_INF)
        s_ctx = _dot_nt(q_pln, kx[:, ks])
        o = _softmax_pv([(s_loc, vcat[:, ks]), (s_ctx, vx[:, ks])], sink)
        outs.extend(o[g * tq:(g + 1) * tq] for g in range(group))
    o_ref[...] = jnp.concatenate(outs, axis=1).astype(o_ref.dtype)


def _window_attention(qr, qp, k, v, kx, vx, sink, tile):
    b, s, qw = qr.shape
    kw = k.shape[2]
    lx = kx.shape[1]
    n_kv = kw // HEAD_DIM
    group = (qw // HEAD_DIM) // n_kv
    r = tile // WINDOW
    last = s // WINDOW - 1
    qspec = pl.BlockSpec((None, tile, qw), lambda bi, i: (bi, i, 0))
    prev = pl.BlockSpec((None, WINDOW, kw), lambda bi, i: (bi, jnp.maximum(i * r - 1, 0), 0))
    cur = pl.BlockSpec((None, tile, kw), lambda bi, i: (bi, i, 0))
    nxt = pl.BlockSpec((None, WINDOW, kw), lambda bi, i: (bi, jnp.minimum((i + 1) * r, last), 0))
    ctx = pl.BlockSpec((None, lx, kw), lambda bi, i: (bi, 0, 0))
    return pl.pallas_call(
        functools.partial(_window_kernel, n_kv=n_kv, group=group),
        grid=(b, s // tile),
        in_specs=[qspec, qspec, prev, cur, nxt, prev, cur, nxt, ctx, ctx, _full(sink.shape)],
        out_specs=qspec,
        out_shape=jax.ShapeDtypeStruct((b, s, qw), BF16),
        compiler_params=_params(2),
        name="window_attention",
    )(qr, qp, k, k, k, v, v, v, kx, vx, sink)


def _nbr_kernel(q_ref, kp_ref, kc_ref, kn_ref, vp_ref, vc_ref, vn_ref, kx_ref, vx_ref, bias_ref,
                o_ref, *, n_heads):
    q = q_ref[...]
    kcat = jnp.concatenate([kp_ref[...], kc_ref[...], kn_ref[...]], axis=0)
    vcat = jnp.concatenate([vp_ref[...], vc_ref[...], vn_ref[...]], axis=0)
    kx = kx_ref[...]
    vx = vx_ref[...]
    outs = []
    for h in range(n_heads):
        hs = slice(h * HEAD_DIM, (h + 1) * HEAD_DIM)
        s_loc = _dot_nt(q[:, hs], kcat[:, hs]) + bias_ref[h]
        s_ctx = _dot_nt(q[:, hs], kx[:, hs])
        outs.append(_softmax_pv([(s_loc, vcat[:, hs]), (s_ctx, vx[:, hs])], None))
    o_ref[...] = jnp.concatenate(outs, axis=1).astype(o_ref.dtype)


def _nbr_bias(rel_bias, rows, tile_rows):
    n_tiles = rows // tile_rows
    kr = NA_ROWS
    qc = jnp.arange(GRID_W)
    kc = jnp.arange(GRID_W)
    dc = jnp.clip(kc[None, :] - qc[:, None], 1 - NA_COLS, NA_COLS - 1) + NA_COLS - 1
    c_start = jnp.clip(qc - NA_COLS // 2, 0, GRID_W - NA_COLS)
    col_ok = (kc[None, :] >= c_start[:, None]) & (kc[None, :] < c_start[:, None] + NA_COLS)
    variants = []
    for ti in (0, 1, n_tiles - 1):
        r = ti * tile_rows + jnp.arange(tile_rows)
        start = jnp.clip(r - kr // 2, 0, rows - kr)
        blk = jnp.arange(3)[:, None] - 1 + ti
        krow = blk * tile_rows + jnp.arange(tile_rows)[None, :]
        blk_ok = (blk >= 0) & (blk < n_tiles)
        row_ok = ((krow[None] >= start[:, None, None]) & (krow[None] < start[:, None, None] + kr)
                  & blk_ok[None])
        dr = jnp.clip(krow[None] - r[:, None, None] + NA_ROWS - 1, 0, 2 * NA_ROWS - 2)
        bias = rel_bias[:, dr[:, None, :, :, None], dc[None, :, None, None, :]]
        ok = row_ok[:, None, :, :, None] & col_ok[None, :, None, None, :]
        bias = jnp.where(ok[None], bias.astype(F32), NEG_INF)
        n_h = rel_bias.shape[0]
        variants.append(bias.reshape(n_h, tile_rows * GRID_W, 3 * tile_rows * GRID_W))
    return jnp.stack(variants)


def _nbr_attention(q, k, v, kx, vx, bias, tile):
    b, s, w = q.shape
    lx = kx.shape[1]
    n_heads = w // HEAD_DIM
    nt = s // tile
    tok = pl.BlockSpec((None, tile, w), lambda bi, i: (bi, i, 0))
    prev = pl.BlockSpec((None, tile, w), lambda bi, i: (bi, jnp.maximum(i - 1, 0), 0))
    nxt = pl.BlockSpec((None, tile, w), lambda bi, i: (bi, jnp.minimum(i + 1, nt - 1), 0))
    ctx = pl.BlockSpec((None, lx, w), lambda bi, i: (bi, 0, 0))
    variant = lambda bi, i: (jnp.where(i == 0, 0, jnp.where(i == nt - 1, 2, 1)), 0, 0, 0)
    bspec = pl.BlockSpec((None, n_heads, tile, 3 * tile), variant)
    return pl.pallas_call(
        functools.partial(_nbr_kernel, n_heads=n_heads),
        grid=(b, nt),
        in_specs=[tok, prev, tok, nxt, prev, tok, nxt, ctx, ctx, bspec],
        out_specs=tok,
        out_shape=jax.ShapeDtypeStruct((b, s, w), BF16),
        compiler_params=_params(2),
        name="nbr_attention",
    )(q, k, k, k, v, v, v, kx, vx, bias)


def _ctx_attn_kernel(q_ref, k_ref, v_ref, sink_ref, o_ref, *, n_kv, group, use_sink):
    q = q_ref[...]
    k = k_ref[...]
    v = v_ref[...]
    l = q.shape[0]
    outs = []
    for hk in range(n_kv):
        ks = slice(hk * HEAD_DIM, (hk + 1) * HEAD_DIM)
        for h in range(hk * group, (hk + 1) * group):
            s = _dot_nt(q[:, h * HEAD_DIM:(h + 1) * HEAD_DIM], k[:, ks])
            sink = jnp.broadcast_to(sink_ref[0:1, h:h + 1], (l, 1)) if use_sink else None
            outs.append(_softmax_pv([(s, v[:, ks])], sink))
    o_ref[...] = jnp.concatenate(outs, axis=1).astype(o_ref.dtype)


def _ctx_attention(q, k, v, sink, use_sink):
    b, l, qw = q.shape
    kw = k.shape[2]
    n_kv = kw // HEAD_DIM
    group = (qw // HEAD_DIM) // n_kv
    return pl.pallas_call(
        functools.partial(_ctx_attn_kernel, n_kv=n_kv, group=group, use_sink=use_sink),
        grid=(b,),
        in_specs=[pl.BlockSpec((None, l, qw), lambda bi: (bi, 0, 0)),
                  pl.BlockSpec((None, l, kw), lambda bi: (bi, 0, 0)),
                  pl.BlockSpec((None, l, kw), lambda bi: (bi, 0, 0)),
                  _full(sink.shape)],
        out_specs=pl.BlockSpec((None, l, qw), lambda bi: (bi, 0, 0)),
        out_shape=jax.ShapeDtypeStruct((b, l, qw), BF16),
        compiler_params=_params(1),
        name="ctx_attention",
    )(q, k, v, sink)


def _merge_kernel(x_ref, mod_ref, g_ref, pp_ref, pc_ref, pn_ref, ab_ref, caw_ref,
                  yb_ref, hf_ref, hb_ref, cg_ref, yd_ref,
                  wgl_ref, bg_ref, wbr_ref, wo_ref, o_ref):
    i = pl.program_id(1)
    nt = pl.num_programs(1)
    x = x_ref[...]
    t, d = x.shape
    h = _adaln(x, g_ref[...], mod_ref[3:4, :], mod_ref[4:5, :]).astype(BF16)
    pc = pc_ref[...]
    prev = jnp.where(i > 0, pp_ref[...], 0.0)
    nxt = jnp.where(i < nt - 1, pn_ref[...], 0.0)
    ext = jnp.concatenate([prev, pc, nxt], axis=0)
    conv = (ext[HALO - 1:HALO - 1 + t] * caw_ref[0:1, :] + pc * caw_ref[1:2, :]
            + ext[HALO + 1:HALO + 1 + t] * caw_ref[2:3, :])
    y_a = (ab_ref[...] * conv).astype(BF16)
    y_c = ((hf_ref[...] + hb_ref[...]) * _gelu_tanh(cg_ref[...])).astype(BF16)
    ys = (y_a, yb_ref[...], y_c, yd_ref[...])
    merged = None
    for n in range(N_BRANCH):
        gate = _sigmoid(_dot(h, wgl_ref[:, n * d:(n + 1) * d]) + bg_ref[0:1, n * d:(n + 1) * d])
        term = gate * _dot(ys[n], wbr_ref[n])
        merged = term if merged is None else merged + term
    out = _dot(merged.astype(BF16), wo_ref[...])
    o_ref[...] = x + mod_ref[5:6, :] * out


def _merge(x, mod, g, pa, ab, caw, yb, hf, hb, cg, yd, wgl, bg, wbr, wo, tile):
    b, s, d = x.shape
    mw = pa.shape[2]
    tok = lambda width: pl.BlockSpec((None, tile, width), lambda bi, i: (bi, i, 0))
    return pl.pallas_call(
        _merge_kernel,
        grid=(b, s // tile),
        in_specs=([tok(d),
                   pl.BlockSpec((None, N_MOD, d), lambda bi, i: (bi, 0, 0)),
                   _full((1, d))]
                  + _halo_specs(tile, mw, s, lambda i: i)
                  + [tok(mw), _full((CONV_K, mw)),
                     tok(mw), tok(mw), tok(mw), tok(mw), tok(mw),
                     _full(wgl.shape), _full(bg.shape), _full(wbr.shape), _full(wo.shape)]),
        out_specs=tok(d),
        out_shape=jax.ShapeDtypeStruct((b, s, d), F32),
        compiler_params=_params(2),
        name="merge",
    )(x, mod, g, pa, pa, pa, ab, caw, yb, hf, hb, cg, yd, wgl, bg, wbr, wo)


def _rope_tables(n_tok, width):
    half = HEAD_DIM // 2
    nf = half // 2
    inv_freq = ROPE_BASE ** (-jnp.arange(nf, dtype=F32) / nf)
    pos = jnp.arange(n_tok)
    ang_r = (pos // GRID_W).astype(F32)[:, None] * inv_freq[None, :]
    ang_c = (pos % GRID_W).astype(F32)[:, None] * inv_freq[None, :]
    cos = jnp.concatenate([jnp.cos(ang_r)] * 2 + [jnp.cos(ang_c)] * 2, axis=1)
    sin = jnp.concatenate([-jnp.sin(ang_r), jnp.sin(ang_r), -jnp.sin(ang_c), jnp.sin(ang_c)], axis=1)
    reps = width // HEAD_DIM
    return jnp.tile(cos, (1, reps)), jnp.tile(sin, (1, reps))


def _block_diag(w):
    n, c, e = w.shape
    eye = jnp.eye(n, dtype=w.dtype)
    return (eye[:, None, :, None] * w[:, :, None, :]).reshape(n * c, n * e)


def _pick_tile(n, target):
    t = min(n, target)
    while n % t:
        t //= 2
    return t


def kernel(x, c, ctx, c_ctx, w_mod, b_mod, norm_g, ffn_w_gate, ffn_w_up, ffn_w_down, w_in, b_gate,
           conv_a_w, qk_norm_g, attn_sink, lru_conv_w, lru_conv_b, lru_w_a, lru_b_a, lru_w_x, lru_b_x,
           lru_lam, na_rel_bias, w_branch, w_out):
    b, s, d = x.shape
    lx = ctx.shape[1]
    depth = w_mod.shape[0]
    mw = d // 4
    kvw = mw // 2
    n_mix = w_in.shape[2] - N_BRANCH * d
    rows = s // GRID_W
    assert s % GRID_W == 0 and rows >= 4 * NA_ROWS and lx % HALO == 0

    tile = _pick_tile(s, 512)
    tile_x = _pick_tile(lx, 512)
    tile_win = _pick_tile(s, 256)
    nbr_rows = 4
    tile_nbr = nbr_rows * GRID_W

    cs = jnp.zeros((8, d), F32).at[:b].set(c).at[b].set(c_ctx)
    mods = _modulation(cs, w_mod, b_mod)

    cos, sin = _rope_tables(s, kvw)
    cos_x, sin_x = jnp.ones((lx, kvw), F32), jnp.zeros((lx, kvw), F32)
    lane = jnp.arange(mw)
    mhead = ((lane[:, None] // HEAD_DIM) == (lane[None, :] // HEAD_DIM)).astype(BF16) * (1.0 / HEAD_DIM)
    mhead = mhead.astype(BF16)

    xc = ctx
    for l in range(depth):
        ctx_out = l < depth - 1
        mod_x = mods[l, :b].reshape(b, N_MOD, d)
        mod_c = jnp.broadcast_to(mods[l, b].reshape(1, N_MOD, d), (b, N_MOD, d))
        g = norm_g[l]
        wg, wu, wd = (ffn_w_gate[l].astype(BF16), ffn_w_up[l].astype(BF16), ffn_w_down[l].astype(BF16))
        w_mix = w_in[l, :, :n_mix].astype(BF16)
        w_gl = w_in[l, :, n_mix:].astype(BF16)
        qkg = jnp.tile(qk_norm_g[l], (1, mw // HEAD_DIM))
        sink = attn_sink[l].reshape(1, -1)
        wa = jnp.stack([_block_diag(lru_w_a[l, dr]) for dr in range(2)]).astype(BF16)
        wx = jnp.stack([_block_diag(lru_w_x[l, dr]) for dr in range(2)]).astype(BF16)
        lru_par = (lru_conv_w[l], lru_conv_b[l].reshape(1, mw), wa, lru_b_a[l], wx, lru_b_x[l], lru_lam[l])
        bias = _nbr_bias(na_rel_bias[l], rows, nbr_rows)
        bg = b_gate[l].reshape(1, -1)
        wbr = w_branch[l].astype(BF16)
        wo = w_out[l].astype(BF16)

        x = _ffn(x, mod_x, g[0:1], wg[0], wu[0], wd[0], 0, tile)
        xc = _ffn(xc, mod_c, g[0:1], wg[0], wu[0], wd[0], 0, tile_x)
        (pa, ab, qr, qp, kr, vb, cxl, cg, dq, dk, dv) = _inproj(
            x, mod_x, g[1:2], w_mix, qkg, mhead, cos, sin, tile)
        (pa_c, ab_c, _, qp_c, k_c, vb_c, cx_c, cg_c, dq_c, dk_c, dv_c) = _inproj(
            xc, mod_c, g[1:2], w_mix, qkg, mhead, cos_x, sin_x, tile_x)

        hf_c, hb_c = _scan(cx_c, jnp.zeros((b, 2, mw), F32), *lru_par, tile_x)
        h0 = jnp.stack([hf_c[:, lx - 1], hb_c[:, 0]], axis=1)
        hf, hb = _scan(cxl, h0, *lru_par, tile)
        yb = _window_attention(qr, qp, kr, vb, k_c, vb_c, sink, tile_win)
        yd = _nbr_attention(dq, dk, dv, dk_c, dv_c, bias, tile_nbr)
        x = _merge(x, mod_x, g[1:2], pa, ab, conv_a_w[l], yb, hf, hb, cg, yd, w_gl, bg, wbr, wo, tile)
        x = _ffn(x, mod_x, g[2:3], wg[1], wu[1], wd[1], 6, tile)
        if ctx_out:
            yb_c = _ctx_attention(qp_c, k_c, vb_c, sink, True)
            yd_c = _ctx_attention(dq_c, dk_c, dv_c, sink, False)
            xc = _merge(xc, mod_c, g[1:2], pa_c, ab_c, conv_a_w[l], yb_c, hf_c, hb_c, cg_c, yd_c,
                        w_gl, bg, wbr, wo, tile_x)
            xc = _ffn(xc, mod_c, g[2:3], wg[1], wu[1], wd[1], 6, tile_x)
    return x
```

```python
import functools

import jax
import jax.numpy as jnp
import numpy as np
from jax import lax
from jax.experimental import pallas as pl
from jax.experimental.pallas import tpu as pltpu

HEAD_DIM = 64
GRID_W = 64
WINDOW = 128
NA_ROWS = 8
NA_COLS = 16
N_BRANCH = 4
CONV_K = 3
LRU_CONV_K = 4
LRU_C = 8.0
ROPE_BASE = 10000.0
EPS = 1e-6
NEG_INF = -1e30
N_MOD = 9

V7X_VMEM_LIMIT_BYTES = 60 * 1024 * 1024
HALO = 8
LANES = 128

F32 = jnp.float32
BF16 = jnp.bfloat16


def _params(n_grid):
    return pltpu.CompilerParams(
        dimension_semantics=("arbitrary",) * n_grid,
        vmem_limit_bytes=V7X_VMEM_LIMIT_BYTES)


def _dot(a, b):
    return jnp.dot(a, b, preferred_element_type=F32)


def _dot_nt(a, b):
    return lax.dot_general(a, b, (((1,), (1,)), ((), ())), preferred_element_type=F32)


def _sigmoid(x):
    return jax.nn.sigmoid(x)


def _gelu_tanh(x):
    return 0.5 * x * (1.0 + jnp.tanh(0.7978845608028654 * (x + 0.044715 * (x * x * x))))


def _adaln(x, g, shift, scale):
    ms = jnp.mean(x * x, axis=-1, keepdims=True)
    return (x * lax.rsqrt(ms + EPS) * g) * (1.0 + scale) + shift


def _full(shape):
    n = len(shape)
    return pl.BlockSpec(shape, lambda *_: (0,) * n)


def _slab(shape, *lead):
    n = len(shape)
    return pl.BlockSpec((None,) * len(lead) + tuple(shape), lambda *_: tuple(lead) + (0,) * n,
                        pipeline_mode=pl.Buffered(1))


def _mod_spec(d, layer, ctx_row):
    if ctx_row is None:
        return pl.BlockSpec((None, None, N_MOD, d), lambda bi, i: (layer, bi, 0, 0))
    return pl.BlockSpec((None, None, N_MOD, d), lambda bi, i: (layer, ctx_row, 0, 0))


def _mod_kernel(c_ref, w_ref, b_ref, o_ref):
    cs = c_ref[...]
    s = cs * _sigmoid(cs)
    o_ref[...] = jnp.dot(s, w_ref[...], preferred_element_type=F32,
                         precision=lax.Precision.HIGHEST) + b_ref[...]


def _modulation(cs, w_mod, b_mod):
    depth, d, n = w_mod.shape
    tn = n // 8
    return pl.pallas_call(
        _mod_kernel,
        grid=(depth, n // tn),
        in_specs=[pl.BlockSpec((8, d), lambda l, j: (0, 0)),
                  pl.BlockSpec((None, d, tn), lambda l, j: (l, 0, j)),
                  pl.BlockSpec((None, 1, tn), lambda l, j: (l, 0, j))],
        out_specs=pl.BlockSpec((None, 8, tn), lambda l, j: (l, 0, j)),
        out_shape=jax.ShapeDtypeStruct((depth, 8, n), F32),
        compiler_params=_params(2),
        name="modulation",
    )(cs, w_mod, b_mod.reshape(depth, 1, n))


def _ffn_kernel(x_ref, mod_ref, g_ref, wg_ref, wu_ref, wd_ref, o_ref, *, row0, chunks):
    x = x_ref[...]
    h = _adaln(x, g_ref[...], mod_ref[row0:row0 + 1, :], mod_ref[row0 + 1:row0 + 2, :]).astype(BF16)
    acc = None
    for (lo, hi) in chunks:
        a = _dot(h, wg_ref[:, lo:hi])
        u = _dot(h, wu_ref[:, lo:hi])
        act = (a * _sigmoid(a) * u).astype(BF16)
        part = _dot(act, wd_ref[lo:hi, :])
        acc = part if acc is None else acc + part
    o_ref[...] = x + (0.5 * mod_ref[row0 + 2:row0 + 3, :]) * acc


def _ffn(x, mods, norm_g, wg, wu, wd, layer, which, ctx_row, tile):
    b, s, d = x.shape
    f = wg.shape[3]
    half = (f // 2 + 255) // 256 * 256
    chunks = ((0, half), (half, f)) if half < f else ((0, f),)
    return pl.pallas_call(
        functools.partial(_ffn_kernel, row0=6 * which, chunks=chunks),
        grid=(b, s // tile),
        in_specs=[pl.BlockSpec((None, tile, d), lambda bi, i: (bi, i, 0)),
                  _mod_spec(d, layer, ctx_row),
                  _slab((1, d), layer, 2 * which),
                  _slab((d, f), layer, which), _slab((d, f), layer, which), _slab((f, d), layer, which)],
        out_specs=pl.BlockSpec((None, tile, d), lambda bi, i: (bi, i, 0)),
        out_shape=jax.ShapeDtypeStruct((b, s, d), F32),
        compiler_params=_params(2),
        name="ffn",
    )(x, mods, norm_g, wg, wu, wd)


def _head_rms(t, m_ref, g):
    n = t.shape[1]
    sq = t * t
    hi = sq.astype(BF16)
    lo = (sq - hi.astype(F32)).astype(BF16)
    m = m_ref[0:n, 0:n]
    ms = _dot(hi, m) + _dot(lo, m)
    return t * lax.rsqrt(ms + EPS) * g


def _rope(t, cos, sin_signed):
    n = t.shape[1]
    lane = lax.broadcasted_iota(jnp.int32, t.shape, 1)
    partner = jnp.where((lane % 32) < 16, pltpu.roll(t, n - 16, 1), pltpu.roll(t, 16, 1))
    return t * cos + partner * sin_signed


def _inproj_kernel(x_ref, mod_ref, g_ref, w_ref, qkg_ref, m_ref, cos_ref, sin_ref,
                   pa_ref, ab_ref, qr_ref, qp_ref, kr_ref, vb_ref, cx_ref, cg_ref,
                   dq_ref, dk_ref, dv_ref, *, mw):
    x = x_ref[...]
    h = _adaln(x, g_ref[...], mod_ref[3:4, :], mod_ref[4:5, :]).astype(BF16)
    u = _dot(h, w_ref[...])
    kvw = mw // 2
    o = 0
    ax = u[:, o:o + mw]; o += mw
    ab = u[:, o:o + mw]; o += mw
    ac = u[:, o:o + mw]; o += mw
    bq = u[:, o:o + mw]; o += mw
    bk = u[:, o:o + kvw]; o += kvw
    bv = u[:, o:o + kvw]; o += kvw
    cx = u[:, o:o + mw]; o += mw
    cg = u[:, o:o + mw]; o += mw
    dq = u[:, o:o + mw]; o += mw
    dk = u[:, o:o + mw]; o += mw
    dv = u[:, o:o + mw]; o += mw
    pa_ref[...] = ac * ax
    ab_ref[...] = ab
    cx_ref[...] = cx
    cg_ref[...] = cg
    cos = cos_ref[...]
    sin = sin_ref[...]
    cos2 = jnp.concatenate([cos, cos], axis=1)
    sin2 = jnp.concatenate([sin, sin], axis=1)
    scale = HEAD_DIM ** -0.5
    q = _head_rms(bq, m_ref, qkg_ref[0:1, :]) * scale
    qp_ref[...] = q.astype(BF16)
    qr_ref[...] = _rope(q, cos2, sin2).astype(BF16)
    k = _head_rms(bk, m_ref, qkg_ref[1:2, 0:kvw])
    kr_ref[...] = _rope(k, cos, sin).astype(BF16)
    vb_ref[...] = bv.astype(BF16)
    dq_ref[...] = (_head_rms(dq, m_ref, qkg_ref[2:3, :]) * scale).astype(BF16)
    dk_ref[...] = _head_rms(dk, m_ref, qkg_ref[3:4, :]).astype(BF16)
    dv_ref[...] = dv.astype(BF16)


def _inproj(x, mods, norm_g, w, qkg, mhead, cos, sin, layer, ctx_row, tile):
    b, s, d = x.shape
    mw = qkg.shape[2]
    kvw = mw // 2
    nw = w.shape[2]
    tok = lambda width: pl.BlockSpec((None, tile, width), lambda bi, i: (bi, i, 0))
    widths = (mw, mw, mw, mw, kvw, kvw, mw, mw, mw, mw, mw)
    dtypes = (F32, F32, BF16, BF16, BF16, BF16, F32, F32, BF16, BF16, BF16)
    return pl.pallas_call(
        functools.partial(_inproj_kernel, mw=mw),
        grid=(b, s // tile),
        in_specs=[tok(d),
                  _mod_spec(d, layer, ctx_row),
                  _slab((1, d), layer, 1), _slab((d, nw), layer), _slab((4, mw), layer), _full((mw, mw)),
                  pl.BlockSpec((tile, kvw), lambda bi, i: (i, 0)),
                  pl.BlockSpec((tile, kvw), lambda bi, i: (i, 0))],
        out_specs=[tok(wd) for wd in widths],
        out_shape=[jax.ShapeDtypeStruct((b, s, wd), dt) for wd, dt in zip(widths, dtypes)],
        compiler_params=_params(2),
        name="inproj",
    )(x, mods, norm_g, w, qkg, mhead, cos, sin)


def _shift_rows(t, d, fill, reverse):
    n = t.shape[0]
    row = lax.broadcasted_iota(jnp.int32, t.shape, 0)
    if reverse:
        return jnp.where(row < n - d, pltpu.roll(t, n - d, 0), fill)
    return jnp.where(row >= d, pltpu.roll(t, d, 0), fill)


def _lru_coeffs(prev_ref, cur_ref, next_ref, has_prev, has_next, cw_ref, cb_ref,
                wa_ref, ba_ref, wx_ref, bx_ref, lam_ref, direction):
    cur = cur_ref[...]
    t = cur.shape[0]
    prev = jnp.where(has_prev, prev_ref[...], 0.0)
    nxt = jnp.where(has_next, next_ref[...], 0.0)
    ext = jnp.concatenate([prev, cur, nxt], axis=0)
    u = cb_ref[...] + cur * cw_ref[2:3, :]
    u = u + ext[HALO - 2:HALO - 2 + t] * cw_ref[0:1, :]
    u = u + ext[HALO - 1:HALO - 1 + t] * cw_ref[1:2, :]
    u = u + ext[HALO + 1:HALO + 1 + t] * cw_ref[3:4, :]
    ub = u.astype(BF16)
    r = _sigmoid(_dot(ub, wa_ref[direction]) + ba_ref[direction:direction + 1, :])
    ig = _sigmoid(_dot(ub, wx_ref[direction]) + bx_ref[direction:direction + 1, :])
    nl = -lam_ref[direction:direction + 1, :]
    softplus = jnp.maximum(nl, 0.0) + jnp.log1p(jnp.exp(-jnp.abs(nl)))
    log_a = (-LRU_C) * r * softplus
    a = jnp.exp(log_a)
    return a, jnp.sqrt(-jnp.tanh(log_a) * (a * a + 1.0)) * (ig * u)


def _scan_tile(a, bb, carry, reverse, sa_ref, sb_ref, sc_ref, out_ref):
    t, w = a.shape
    g = t // HALO
    a3 = a.reshape(g, HALO, w)
    b3 = bb.reshape(g, HALO, w)
    sub = lax.broadcasted_iota(jnp.int32, (g, HALO, w), 1)
    d = 1
    while d < HALO:
        ok = (sub < HALO - d) if reverse else (sub >= d)
        shift = HALO - d if reverse else d
        a_r = pltpu.roll(a3, shift, 1)
        b_r = pltpu.roll(b3, shift, 1)
        b3 = jnp.where(ok, a3 * b_r + b3, b3)
        a3 = jnp.where(ok, a3 * a_r, a3)
        d *= 2
    sa_ref[...] = a3.reshape(t, w)
    sb_ref[...] = b3.reshape(t, w)
    edge = 0 if reverse else HALO - 1
    at = sa_ref[pl.ds(edge, g, stride=HALO), :]
    bt = sb_ref[pl.ds(edge, g, stride=HALO), :]
    d = 1
    while d < g:
        a_s = _shift_rows(at, d, 1.0, reverse)
        b_s = _shift_rows(bt, d, 0.0, reverse)
        bt = at * b_s + bt
        at = at * a_s
        d *= 2
    h_end = at * carry + bt
    sc_ref[...] = _shift_rows(h_end, 1, carry, reverse)
    for gi in range(g):
        rows = slice(gi * HALO, (gi + 1) * HALO)
        out_ref[rows, :] = sa_ref[rows, :] * sc_ref[gi:gi + 1, :] + sb_ref[rows, :]
    return h_end[0:1, :] if reverse else h_end[g - 1:g, :]


def _scan_kernel(fp_ref, fc_ref, fn_ref, rp_ref, rc_ref, rn_ref, h0_ref, cw_ref, cb_ref,
                 wa_ref, ba_ref, wx_ref, bx_ref, lam_ref, hf_ref, hb_ref,
                 carry_ref, sa_ref, sb_ref, sc_ref):
    i = pl.program_id(1)
    nt = pl.num_programs(1)

    @pl.when(i == 0)
    def _():
        carry_ref[...] = h0_ref[...]

    par = (cw_ref, cb_ref, wa_ref, ba_ref, wx_ref, bx_ref, lam_ref)
    tiles = ((fp_ref, fc_ref, fn_ref, i > 0, i < nt - 1, hf_ref),
             (rp_ref, rc_ref, rn_ref, i < nt - 1, i > 0, hb_ref))
    for direction, (p_ref, c_ref, n_ref, has_prev, has_next, out_ref) in enumerate(tiles):
        a, bb = _lru_coeffs(p_ref, c_ref, n_ref, has_prev, has_next, *par, direction)
        for j in range(a.shape[1] // LANES):
            cols = slice(j * LANES, (j + 1) * LANES)
            carry_ref[direction:direction + 1, cols] = _scan_tile(
                a[:, cols], bb[:, cols], carry_ref[direction:direction + 1, cols], direction == 1,
                sa_ref.at[direction, j], sb_ref.at[direction, j], sc_ref.at[direction, j],
                out_ref.at[:, cols])


def _halo_specs(tile, width, n_rows, tile_index):
    r = tile // HALO
    last = n_rows // HALO - 1
    prev = pl.BlockSpec((None, HALO, width),
                        lambda bi, i: (bi, jnp.maximum(tile_index(i) * r - 1, 0), 0))
    cur = pl.BlockSpec((None, tile, width), lambda bi, i: (bi, tile_index(i), 0))
    nxt = pl.BlockSpec((None, HALO, width),
                       lambda bi, i: (bi, jnp.minimum((tile_index(i) + 1) * r, last), 0))
    return [prev, cur, nxt]


def _scan(cx, h0, cw, cb, wa, ba, wx, bx, lam, layer, tile):
    b, s, w = cx.shape
    nt = s // tile
    fwd = lambda i: i
    rev = lambda i: nt - 1 - i
    return pl.pallas_call(
        _scan_kernel,
        grid=(b, nt),
        in_specs=(_halo_specs(tile, w, s, fwd) + _halo_specs(tile, w, s, rev)
                  + [pl.BlockSpec((None, 2, w), lambda bi, i: (bi, 0, 0)),
                     _slab((LRU_CONV_K, w), layer), _slab((1, w), layer),
                     _slab((2, w, w), layer), _slab((2, w), layer),
                     _slab((2, w, w), layer), _slab((2, w), layer), _slab((2, w), layer)]),
        out_specs=[pl.BlockSpec((None, tile, w), lambda bi, i: (bi, i, 0)),
                   pl.BlockSpec((None, tile, w), lambda bi, i: (bi, nt - 1 - i, 0))],
        out_shape=[jax.ShapeDtypeStruct((b, s, w), F32)] * 2,
        scratch_shapes=[pltpu.VMEM((2, w), F32),
                        pltpu.VMEM((2, w // LANES, tile, LANES), F32),
                        pltpu.VMEM((2, w // LANES, tile, LANES), F32),
                        pltpu.VMEM((2, w // LANES, tile // HALO, LANES), F32)],
        compiler_params=_params(2),
        name="lru_scan",
    )(cx, cx, cx, cx, cx, cx, h0, cw, cb, wa, ba, wx, bx, lam)


def _softmax_pv(parts, sink):
    m = None
    for s, _ in parts:
        mx = jnp.max(s, axis=-1, keepdims=True)
        m = mx if m is None else jnp.maximum(m, mx)
    if sink is not None:
        m = jnp.maximum(m, sink)
    den = None
    out = None
    for s, v in parts:
        p = jnp.exp(s - m)
        sm = jnp.sum(p, axis=-1, keepdims=True)
        den = sm if den is None else den + sm
        pv = _dot(p.astype(BF16), v)
        out = pv if out is None else out + pv
    if sink is not None:
        den = den + jnp.exp(sink - m)
    return out / den


def _window_kernel(qr_ref, qp_ref, kp_ref, kc_ref, kn_ref, vp_ref, vc_ref, vn_ref,
                   kx_ref, vx_ref, sink_ref, o_ref, *, n_kv, group):
    i = pl.program_id(1)
    nt = pl.num_programs(1)
    tq = qr_ref.shape[0]
    nk = tq + 2 * WINDOW
    rows = group * tq
    r = lax.broadcasted_iota(jnp.int32, (rows, nk), 0) % tq
    kk = lax.broadcasted_iota(jnp.int32, (rows, nk), 1)
    rel = kk - WINDOW - r
    k_lo = jnp.where(i > 0, 0, WINDOW)
    k_hi = jnp.where(i < nt - 1, nk, tq + WINDOW)
    valid = (rel <= WINDOW) & (rel >= -WINDOW) & (kk >= k_lo) & (kk < k_hi)
    qr = qr_ref[...]
    qp = qp_ref[...]
    kcat = jnp.concatenate([kp_ref[...], kc_ref[...], kn_ref[...]], axis=0)
    vcat = jnp.concatenate([vp_ref[...], vc_ref[...], vn_ref[...]], axis=0)
    kx = kx_ref[...]
    vx = vx_ref[...]
    outs = []
    for hk in range(n_kv):
        ks = slice(hk * HEAD_DIM, (hk + 1) * HEAD_DIM)
        heads = range(hk * group, (hk + 1) * group)
        q_rot = jnp.concatenate([qr[:, h * HEAD_DIM:(h + 1) * HEAD_DIM] for h in heads], axis=0)
        q_pln = jnp.concatenate([qp[:, h * HEAD_DIM:(h + 1) * HEAD_DIM] for h in heads], axis=0)
        sink = jnp.concatenate(
            [jnp.broadcast_to(sink_ref[0:1, h:h + 1], (tq, 1)) for h in heads], axis=0)
        s_loc = jnp.where(valid, _dot_nt(q_rot, kcat[:, ks]), NEG_INF)
        s_ctx = _dot_nt(q_pln, kx[:, ks])
        o = _softmax_pv([(s_loc, vcat[:, ks]), (s_ctx, vx[:, ks])], sink)
        outs.extend(o[g * tq:(g + 1) * tq] for g in range(group))
    o_ref[...] = jnp.concatenate(outs, axis=1).astype(o_ref.dtype)


def _window_attention(qr, qp, k, v, kx, vx, sink, layer, tile):
    b, s, qw = qr.shape
    kw = k.shape[2]
    lx = kx.shape[1]
    n_kv = kw // HEAD_DIM
    group = (qw // HEAD_DIM) // n_kv
    r = tile // WINDOW
    last = s // WINDOW - 1
    qspec = pl.BlockSpec((None, tile, qw), lambda bi, i: (bi, i, 0))
    prev = pl.BlockSpec((None, WINDOW, kw), lambda bi, i: (bi, jnp.maximum(i * r - 1, 0), 0))
    cur = pl.BlockSpec((None, tile, kw), lambda bi, i: (bi, i, 0))
    nxt = pl.BlockSpec((None, WINDOW, kw), lambda bi, i: (bi, jnp.minimum((i + 1) * r, last), 0))
    ctx = pl.BlockSpec((None, lx, kw), lambda bi, i: (bi, 0, 0))
    return pl.pallas_call(
        functools.partial(_window_kernel, n_kv=n_kv, group=group),
        grid=(b, s // tile),
        in_specs=[qspec, qspec, prev, cur, nxt, prev, cur, nxt, ctx, ctx, _slab(sink.shape[1:], layer)],
        out_specs=qspec,
        out_shape=jax.ShapeDtypeStruct((b, s, qw), BF16),
        compiler_params=_params(2),
        name="window_attention",
    )(qr, qp, k, k, k, v, v, v, kx, vx, sink)


def _nbr_kernel(q_ref, kp_ref, kc_ref, kn_ref, vp_ref, vc_ref, vn_ref, kx_ref, vx_ref, bias_ref,
                o_ref, *, n_heads):
    q = q_ref[...]
    kcat = jnp.concatenate([kp_ref[...], kc_ref[...], kn_ref[...]], axis=0)
    vcat = jnp.concatenate([vp_ref[...], vc_ref[...], vn_ref[...]], axis=0)
    kx = kx_ref[...]
    vx = vx_ref[...]
    outs = []
    for h in range(n_heads):
        hs = slice(h * HEAD_DIM, (h + 1) * HEAD_DIM)
        s_loc = _dot_nt(q[:, hs], kcat[:, hs]) + bias_ref[h]
        s_ctx = _dot_nt(q[:, hs], kx[:, hs])
        outs.append(_softmax_pv([(s_loc, vcat[:, hs]), (s_ctx, vx[:, hs])], None))
    o_ref[...] = jnp.concatenate(outs, axis=1).astype(o_ref.dtype)


def _nbr_bias(rel_bias, rows, tile_rows):
    n_tiles = rows // tile_rows
    kr = NA_ROWS
    qc = np.arange(GRID_W)
    kc = np.arange(GRID_W)
    dc = np.clip(kc[None, :] - qc[:, None], 1 - NA_COLS, NA_COLS - 1) + NA_COLS - 1
    c_start = np.clip(qc - NA_COLS // 2, 0, GRID_W - NA_COLS)
    col_ok = (kc[None, :] >= c_start[:, None]) & (kc[None, :] < c_start[:, None] + NA_COLS)
    dr_all, ok_all = [], []
    for ti in (0, 1, n_tiles - 1):
        r = ti * tile_rows + np.arange(tile_rows)
        start = np.clip(r - kr // 2, 0, rows - kr)
        blk = np.arange(3)[:, None] - 1 + ti
        krow = blk * tile_rows + np.arange(tile_rows)[None, :]
        blk_ok = (blk >= 0) & (blk < n_tiles)
        ok_all.append((krow[None] >= start[:, None, None]) & (krow[None] < start[:, None, None] + kr)
                      & blk_ok[None])
        dr_all.append(np.clip(krow[None] - r[:, None, None] + NA_ROWS - 1, 0, 2 * NA_ROWS - 2))
    oh_dc = jnp.asarray(np.eye(2 * NA_COLS - 1, dtype=np.float32)[dc])
    oh_dr = jnp.asarray(np.eye(2 * NA_ROWS - 1, dtype=np.float32)[np.stack(dr_all)])
    hi = lax.Precision.HIGHEST
    toe = jnp.einsum('qkc,lhrc->lhrqk', oh_dc, rel_bias.astype(F32), precision=hi)
    bias = jnp.einsum('vpbjr,lhrqk->lvhpqbjk', oh_dr, toe, precision=hi)
    ok = (jnp.asarray(np.stack(ok_all))[:, None, :, None, :, :, None]
          & jnp.asarray(col_ok)[None, None, None, :, None, None, :])
    bias = jnp.where(ok[None], bias, NEG_INF)
    n_l, n_h = rel_bias.shape[:2]
    return bias.reshape(n_l, 3, n_h, tile_rows * GRID_W, 3 * tile_rows * GRID_W)


def _nbr_attention(q, k, v, kx, vx, bias, layer, tile):
    b, s, w = q.shape
    lx = kx.shape[1]
    n_heads = w // HEAD_DIM
    nt = s // tile
    tok = pl.BlockSpec((None, tile, w), lambda bi, i: (bi, i, 0))
    prev = pl.BlockSpec((None, tile, w), lambda bi, i: (bi, jnp.maximum(i - 1, 0), 0))
    nxt = pl.BlockSpec((None, tile, w), lambda bi, i: (bi, jnp.minimum(i + 1, nt - 1), 0))
    ctx = pl.BlockSpec((None, lx, w), lambda bi, i: (bi, 0, 0))
    variant = lambda bi, i: (layer, jnp.where(i == 0, 0, jnp.where(i == nt - 1, 2, 1)), 0, 0, 0)
    bspec = pl.BlockSpec((None, None, n_heads, tile, 3 * tile), variant)
    return pl.pallas_call(
        functools.partial(_nbr_kernel, n_heads=n_heads),
        grid=(b, nt),
        in_specs=[tok, prev, tok, nxt, prev, tok, nxt, ctx, ctx, bspec],
        out_specs=tok,
        out_shape=jax.ShapeDtypeStruct((b, s, w), BF16),
        compiler_params=_params(2),
        name="nbr_attention",
    )(q, k, k, k, v, v, v, kx, vx, bias)


def _ctx_attn_kernel(q_ref, k_ref, v_ref, sink_ref, o_ref, *, n_kv, group, use_sink):
    q = q_ref[...]
    k = k_ref[...]
    v = v_ref[...]
    l = q.shape[0]
    outs = []
    for hk in range(n_kv):
        ks = slice(hk * HEAD_DIM, (hk + 1) * HEAD_DIM)
        for h in range(hk * group, (hk + 1) * group):
            s = _dot_nt(q[:, h * HEAD_DIM:(h + 1) * HEAD_DIM], k[:, ks])
            sink = jnp.broadcast_to(sink_ref[0:1, h:h + 1], (l, 1)) if use_sink else None
            outs.append(_softmax_pv([(s, v[:, ks])], sink))
    o_ref[...] = jnp.concatenate(outs, axis=1).astype(o_ref.dtype)


def _ctx_attention(q, k, v, sink, layer, use_sink):
    b, l, qw = q.shape
    kw = k.shape[2]
    n_kv = kw // HEAD_DIM
    group = (qw // HEAD_DIM) // n_kv
    return pl.pallas_call(
        functools.partial(_ctx_attn_kernel, n_kv=n_kv, group=group, use_sink=use_sink),
        grid=(b,),
        in_specs=[pl.BlockSpec((None, l, qw), lambda bi: (bi, 0, 0)),
                  pl.BlockSpec((None, l, kw), lambda bi: (bi, 0, 0)),
                  pl.BlockSpec((None, l, kw), lambda bi: (bi, 0, 0)),
                  _slab(sink.shape[1:], layer)],
        out_specs=pl.BlockSpec((None, l, qw), lambda bi: (bi, 0, 0)),
        out_shape=jax.ShapeDtypeStruct((b, l, qw), BF16),
        compiler_params=_params(1),
        name="ctx_attention",
    )(q, k, v, sink)


def _merge_kernel(x_ref, mod_ref, g_ref, pp_ref, pc_ref, pn_ref, ab_ref, caw_ref,
                  yb_ref, hf_ref, hb_ref, cg_ref, yd_ref,
                  wgl_ref, bg_ref, wbr_ref, wo_ref, o_ref):
    i = pl.program_id(1)
    nt = pl.num_programs(1)
    x = x_ref[...]
    t, d = x.shape
    h = _adaln(x, g_ref[...], mod_ref[3:4, :], mod_ref[4:5, :]).astype(BF16)
    pc = pc_ref[...]
    prev = jnp.where(i > 0, pp_ref[...], 0.0)
    nxt = jnp.where(i < nt - 1, pn_ref[...], 0.0)
    ext = jnp.concatenate([prev, pc, nxt], axis=0)
    conv = (ext[HALO - 1:HALO - 1 + t] * caw_ref[0:1, :] + pc * caw_ref[1:2, :]
            + ext[HALO + 1:HALO + 1 + t] * caw_ref[2:3, :])
    y_a = (ab_ref[...] * conv).astype(BF16)
    y_c = ((hf_ref[...] + hb_ref[...]) * _gelu_tanh(cg_ref[...])).astype(BF16)
    ys = (y_a, yb_ref[...], y_c, yd_ref[...])
    merged = None
    for n in range(N_BRANCH):
        gate = _sigmoid(_dot(h, wgl_ref[:, n * d:(n + 1) * d]) + bg_ref[0:1, n * d:(n + 1) * d])
        term = gate * _dot(ys[n], wbr_ref[n])
        merged = term if merged is None else merged + term
    out = _dot(merged.astype(BF16), wo_ref[...])
    o_ref[...] = x + mod_ref[5:6, :] * out


def _merge(x, mods, norm_g, pa, ab, caw, yb, hf, hb, cg, yd, wgl, bg, wbr, wo, layer, ctx_row, tile):
    b, s, d = x.shape
    mw = pa.shape[2]
    tok = lambda width: pl.BlockSpec((None, tile, width), lambda bi, i: (bi, i, 0))
    return pl.pallas_call(
        _merge_kernel,
        grid=(b, s // tile),
        in_specs=([tok(d), _mod_spec(d, layer, ctx_row), _slab((1, d), layer, 1)]
                  + _halo_specs(tile, mw, s, lambda i: i)
                  + [tok(mw), _slab((CONV_K, mw), layer),
                     tok(mw), tok(mw), tok(mw), tok(mw), tok(mw),
                     _slab(wgl.shape[1:], layer), _slab(bg.shape[1:], layer),
                     _slab(wbr.shape[1:], layer), _slab(wo.shape[1:], layer)]),
        out_specs=tok(d),
        out_shape=jax.ShapeDtypeStruct((b, s, d), F32),
        compiler_params=_params(2),
        name="merge",
    )(x, mods, norm_g, pa, pa, pa, ab, caw, yb, hf, hb, cg, yd, wgl, bg, wbr, wo)


def _rope_tables(n_tok, width):
    half = HEAD_DIM // 2
    nf = half // 2
    inv_freq = ROPE_BASE ** (-jnp.arange(nf, dtype=F32) / nf)
    pos = jnp.arange(n_tok)
    ang_r = (pos // GRID_W).astype(F32)[:, None] * inv_freq[None, :]
    ang_c = (pos % GRID_W).astype(F32)[:, None] * inv_freq[None, :]
    cos = jnp.concatenate([jnp.cos(ang_r)] * 2 + [jnp.cos(ang_c)] * 2, axis=1)
    sin = jnp.concatenate([-jnp.sin(ang_r), jnp.sin(ang_r), -jnp.sin(ang_c), jnp.sin(ang_c)], axis=1)
    reps = width // HEAD_DIM
    return jnp.tile(cos, (1, reps)), jnp.tile(sin, (1, reps))


def _block_diag(w):
    *lead, n, c, e = w.shape
    eye = jnp.eye(n, dtype=w.dtype)
    return (eye[:, None, :, None] * w[..., :, :, None, :]).reshape(*lead, n * c, n * e)


def _pick_tile(n, target):
    t = min(n, target)
    while n % t:
        t //= 2
    return t


def kernel(x, c, ctx, c_ctx, w_mod, b_mod, norm_g, ffn_w_gate, ffn_w_up, ffn_w_down, w_in, b_gate,
           conv_a_w, qk_norm_g, attn_sink, lru_conv_w, lru_conv_b, lru_w_a, lru_b_a, lru_w_x, lru_b_x,
           lru_lam, na_rel_bias, w_branch, w_out):
    b, s, d = x.shape
    lx = ctx.shape[1]
    depth = w_mod.shape[0]
    mw = d // 4
    kvw = mw // 2
    n_mix = w_in.shape[2] - N_BRANCH * d
    rows = s // GRID_W
    assert s % GRID_W == 0 and rows >= 4 * NA_ROWS and lx % HALO == 0

    tile = _pick_tile(s, 512)
    tile_x = _pick_tile(lx, 512)
    tile_win = _pick_tile(s, 256)
    nbr_rows = 4
    tile_nbr = nbr_rows * GRID_W

    cs = jnp.zeros((8, d), F32).at[:b].set(c).at[b].set(c_ctx)
    mods = _modulation(cs, w_mod, b_mod).reshape(depth, 8, N_MOD, d)
    lat, cx_row = None, b

    cos, sin = _rope_tables(s, kvw)
    cos_x, sin_x = jnp.ones((lx, kvw), F32), jnp.zeros((lx, kvw), F32)
    lane = np.arange(mw)
    mhead = jnp.asarray((lane[:, None] // HEAD_DIM == lane[None, :] // HEAD_DIM) / HEAD_DIM, dtype=BF16)

    ng = norm_g.reshape(depth, 3, 1, d)
    wg, wu, wd = ffn_w_gate.astype(BF16), ffn_w_up.astype(BF16), ffn_w_down.astype(BF16)
    w_mix = w_in[:, :, :n_mix].astype(BF16)
    w_gl = w_in[:, :, n_mix:].astype(BF16)
    qkg = jnp.tile(qk_norm_g, (1, 1, mw // HEAD_DIM))
    sink = attn_sink.reshape(depth, 1, -1)
    lru_par = (lru_conv_w, lru_conv_b.reshape(depth, 1, mw), _block_diag(lru_w_a).astype(BF16), lru_b_a,
               _block_diag(lru_w_x).astype(BF16), lru_b_x, lru_lam)
    bias = _nbr_bias(na_rel_bias, rows, nbr_rows)
    bg = b_gate.reshape(depth, 1, -1)
    wbr = w_branch.astype(BF16)
    wo = w_out.astype(BF16)

    xc = ctx
    for l in range(depth):
        ctx_out = l < depth - 1
        x = _ffn(x, mods, ng, wg, wu, wd, l, 0, lat, tile)
        xc = _ffn(xc, mods, ng, wg, wu, wd, l, 0, cx_row, tile_x)
        (pa, ab, qr, qp, kr, vb, cxl, cg, dq, dk, dv) = _inproj(
            x, mods, ng, w_mix, qkg, mhead, cos, sin, l, lat, tile)
        (pa_c, ab_c, _, qp_c, k_c, vb_c, cx_c, cg_c, dq_c, dk_c, dv_c) = _inproj(
            xc, mods, ng, w_mix, qkg, mhead, cos_x, sin_x, l, cx_row, tile_x)

        hf_c, hb_c = _scan(cx_c, jnp.zeros((b, 2, mw), F32), *lru_par, l, tile_x)
        h0 = jnp.stack([hf_c[:, lx - 1], hb_c[:, 0]], axis=1)
        hf, hb = _scan(cxl, h0, *lru_par, l, tile)
        yb = _window_attention(qr, qp, kr, vb, k_c, vb_c, sink, l, tile_win)
        yd = _nbr_attention(dq, dk, dv, dk_c, dv_c, bias, l, tile_nbr)
        x = _merge(x, mods, ng, pa, ab, conv_a_w, yb, hf, hb, cg, yd, w_gl, bg, wbr, wo, l, lat, tile)
        x = _ffn(x, mods, ng, wg, wu, wd, l, 1, lat, tile)
        if ctx_out:
            yb_c = _ctx_attention(qp_c, k_c, vb_c, sink, l, True)
            yd_c = _ctx_attention(dq_c, dk_c, dv_c, sink, l, False)
            xc = _merge(xc, mods, ng, pa_c, ab_c, conv_a_w, yb_c, hf_c, hb_c, cg_c, yd_c,
                        w_gl, bg, wbr, wo, l, cx_row, tile_x)
            xc = _ffn(xc, mods, ng, wg, wu, wd, l, 1, cx_row, tile_x)
    return x
```

```python
import functools

import jax
import jax.numpy as jnp
import numpy as np
from jax import lax
from jax.experimental import pallas as pl
from jax.experimental.pallas import tpu as pltpu

HEAD_DIM = 64
GRID_W = 64
WINDOW = 128
NA_ROWS = 8
NA_COLS = 16
N_BRANCH = 4
CONV_K = 3
LRU_CONV_K = 4
LRU_C = 8.0
ROPE_BASE = 10000.0
EPS = 1e-6
NEG_INF = -1e30
N_MOD = 9
LOG2E = 1.4426950408889634

V7X_VMEM_LIMIT_BYTES = 60 * 1024 * 1024
HALO = 8
LANES = 128

F32 = jnp.float32
BF16 = jnp.bfloat16


def _params(n_grid):
    return pltpu.CompilerParams(
        dimension_semantics=("arbitrary",) * n_grid,
        vmem_limit_bytes=V7X_VMEM_LIMIT_BYTES)


def _dot(a, b):
    return jnp.dot(a, b, preferred_element_type=F32)


def _dot_nt(a, b):
    return lax.dot_general(a, b, (((1,), (1,)), ((), ())), preferred_element_type=F32)


def _sigmoid(x):
    return jax.nn.sigmoid(x)


def _gelu_tanh(x):
    return 0.5 * x * (1.0 + jnp.tanh(0.7978845608028654 * (x + 0.044715 * (x * x * x))))


def _adaln(x, g, shift, scale):
    ms = jnp.mean(x * x, axis=-1, keepdims=True)
    return (x * lax.rsqrt(ms + EPS) * g) * (1.0 + scale) + shift


def _full(shape):
    n = len(shape)
    return pl.BlockSpec(shape, lambda *_: (0,) * n)


def _slab(shape, *lead):
    n = len(shape)
    return pl.BlockSpec((None,) * len(lead) + tuple(shape), lambda *_: tuple(lead) + (0,) * n,
                        pipeline_mode=pl.Buffered(1))


def _mod_spec(d, layer, ctx_row):
    if ctx_row is None:
        return pl.BlockSpec((None, None, N_MOD, d), lambda bi, i: (layer, bi, 0, 0))
    return pl.BlockSpec((None, None, N_MOD, d), lambda bi, i: (layer, ctx_row, 0, 0))


def _mod_kernel(c_ref, w_ref, b_ref, o_ref):
    cs = c_ref[...]
    s = cs * _sigmoid(cs)
    o_ref[...] = jnp.dot(s, w_ref[...], preferred_element_type=F32,
                         precision=lax.Precision.HIGHEST) + b_ref[...]


def _modulation(cs, w_mod, b_mod):
    depth, d, n = w_mod.shape
    tn = n // 8
    return pl.pallas_call(
        _mod_kernel,
        grid=(depth, n // tn),
        in_specs=[pl.BlockSpec((8, d), lambda l, j: (0, 0)),
                  pl.BlockSpec((None, d, tn), lambda l, j: (l, 0, j)),
                  pl.BlockSpec((None, 1, tn), lambda l, j: (l, 0, j))],
        out_specs=pl.BlockSpec((None, 8, tn), lambda l, j: (l, 0, j)),
        out_shape=jax.ShapeDtypeStruct((depth, 8, n), F32),
        compiler_params=_params(2),
        name="modulation",
    )(cs, w_mod, b_mod.reshape(depth, 1, n))


def _ffn_kernel(x_ref, mod_ref, g_ref, wg_ref, wu_ref, wd_ref, o_ref, *, row0, chunks):
    x = x_ref[...]
    h = _adaln(x, g_ref[...], mod_ref[row0:row0 + 1, :], mod_ref[row0 + 1:row0 + 2, :]).astype(BF16)
    acc = None
    for (lo, hi) in chunks:
        a = _dot(h, wg_ref[:, lo:hi])
        u = _dot(h, wu_ref[:, lo:hi])
        act = (a * _sigmoid(a) * u).astype(BF16)
        part = _dot(act, wd_ref[lo:hi, :])
        acc = part if acc is None else acc + part
    o_ref[...] = x + (0.5 * mod_ref[row0 + 2:row0 + 3, :]) * acc


def _ffn(x, mods, norm_g, wg, wu, wd, layer, which, ctx_row, tile):
    b, s, d = x.shape
    f = wg.shape[3]
    half = (f // 2 + 255) // 256 * 256
    chunks = ((0, half), (half, f)) if half < f else ((0, f),)
    return pl.pallas_call(
        functools.partial(_ffn_kernel, row0=6 * which, chunks=chunks),
        grid=(b, s // tile),
        in_specs=[pl.BlockSpec((None, tile, d), lambda bi, i: (bi, i, 0)),
                  _mod_spec(d, layer, ctx_row),
                  _slab((1, d), layer, 2 * which),
                  _slab((d, f), layer, which), _slab((d, f), layer, which), _slab((f, d), layer, which)],
        out_specs=pl.BlockSpec((None, tile, d), lambda bi, i: (bi, i, 0)),
        out_shape=jax.ShapeDtypeStruct((b, s, d), F32),
        compiler_params=_params(2),
        name="ffn",
    )(x, mods, norm_g, wg, wu, wd)


def _head_rms(t, m_ref, g):
    n = t.shape[1]
    sq = t * t
    hi = sq.astype(BF16)
    lo = (sq - hi.astype(F32)).astype(BF16)
    m = m_ref[0:n, 0:n]
    ms = _dot(hi, m) + _dot(lo, m)
    return t * lax.rsqrt(ms + EPS) * g


def _rope(t, cos, sin_signed):
    n = t.shape[1]
    lane = lax.broadcasted_iota(jnp.int32, t.shape, 1)
    partner = jnp.where((lane % 32) < 16, pltpu.roll(t, n - 16, 1), pltpu.roll(t, 16, 1))
    return t * cos + partner * sin_signed


def _inproj_kernel(x_ref, mod_ref, g_ref, w_ref, qkg_ref, m_ref, cos_ref, sin_ref,
                   pa_ref, ab_ref, qr_ref, qp_ref, kr_ref, vb_ref, cx_ref, cg_ref,
                   dq_ref, dk_ref, dv_ref, *, mw):
    x = x_ref[...]
    h = _adaln(x, g_ref[...], mod_ref[3:4, :], mod_ref[4:5, :]).astype(BF16)
    u = _dot(h, w_ref[...])
    kvw = mw // 2
    o = 0
    ax = u[:, o:o + mw]; o += mw
    ab = u[:, o:o + mw]; o += mw
    ac = u[:, o:o + mw]; o += mw
    bq = u[:, o:o + mw]; o += mw
    bk = u[:, o:o + kvw]; o += kvw
    bv = u[:, o:o + kvw]; o += kvw
    cx = u[:, o:o + mw]; o += mw
    cg = u[:, o:o + mw]; o += mw
    dq = u[:, o:o + mw]; o += mw
    dk = u[:, o:o + mw]; o += mw
    dv = u[:, o:o + mw]; o += mw
    pa_ref[...] = ac * ax
    ab_ref[...] = ab
    cx_ref[...] = cx
    cg_ref[...] = cg
    cos = cos_ref[...]
    sin = sin_ref[...]
    cos2 = jnp.concatenate([cos, cos], axis=1)
    sin2 = jnp.concatenate([sin, sin], axis=1)
    scale = LOG2E * HEAD_DIM ** -0.5
    q = _head_rms(bq, m_ref, qkg_ref[0:1, :]) * scale
    qp_ref[...] = q.astype(BF16)
    qr_ref[...] = _rope(q, cos2, sin2).astype(BF16)
    k = _head_rms(bk, m_ref, qkg_ref[1:2, 0:kvw])
    kr_ref[...] = _rope(k, cos, sin).astype(BF16)
    vb_ref[...] = bv.astype(BF16)
    dq_ref[...] = (_head_rms(dq, m_ref, qkg_ref[2:3, :]) * scale).astype(BF16)
    dk_ref[...] = _head_rms(dk, m_ref, qkg_ref[3:4, :]).astype(BF16)
    dv_ref[...] = dv.astype(BF16)


def _inproj(x, mods, norm_g, w, qkg, mhead, cos, sin, layer, ctx_row, tile):
    b, s, d = x.shape
    mw = qkg.shape[2]
    kvw = mw // 2
    nw = w.shape[2]
    tok = lambda width: pl.BlockSpec((None, tile, width), lambda bi, i: (bi, i, 0))
    widths = (mw, mw, mw, mw, kvw, kvw, mw, mw, mw, mw, mw)
    dtypes = (F32, F32, BF16, BF16, BF16, BF16, F32, F32, BF16, BF16, BF16)
    return pl.pallas_call(
        functools.partial(_inproj_kernel, mw=mw),
        grid=(b, s // tile),
        in_specs=[tok(d),
                  _mod_spec(d, layer, ctx_row),
                  _slab((1, d), layer, 1), _slab((d, nw), layer), _slab((4, mw), layer), _full((mw, mw)),
                  pl.BlockSpec((tile, kvw), lambda bi, i: (i, 0)),
                  pl.BlockSpec((tile, kvw), lambda bi, i: (i, 0))],
        out_specs=[tok(wd) for wd in widths],
        out_shape=[jax.ShapeDtypeStruct((b, s, wd), dt) for wd, dt in zip(widths, dtypes)],
        compiler_params=_params(2),
        name="inproj",
    )(x, mods, norm_g, w, qkg, mhead, cos, sin)


def _shift_rows(t, d, fill, reverse):
    n = t.shape[0]
    row = lax.broadcasted_iota(jnp.int32, t.shape, 0)
    if reverse:
        return jnp.where(row < n - d, pltpu.roll(t, n - d, 0), fill)
    return jnp.where(row >= d, pltpu.roll(t, d, 0), fill)


def _lru_coeffs(prev_ref, cur_ref, next_ref, has_prev, has_next, cw_ref, cb_ref,
                wa_ref, ba_ref, wx_ref, bx_ref, lam_ref, direction):
    cur = cur_ref[...]
    t = cur.shape[0]
    prev = jnp.where(has_prev, prev_ref[...], 0.0)
    nxt = jnp.where(has_next, next_ref[...], 0.0)
    ext = jnp.concatenate([prev, cur, nxt], axis=0)
    u = cb_ref[...] + cur * cw_ref[2:3, :]
    u = u + ext[HALO - 2:HALO - 2 + t] * cw_ref[0:1, :]
    u = u + ext[HALO - 1:HALO - 1 + t] * cw_ref[1:2, :]
    u = u + ext[HALO + 1:HALO + 1 + t] * cw_ref[3:4, :]
    ub = u.astype(BF16)
    r = _sigmoid(_dot(ub, wa_ref[direction]) + ba_ref[direction:direction + 1, :])
    ig = _sigmoid(_dot(ub, wx_ref[direction]) + bx_ref[direction:direction + 1, :])
    nl = -lam_ref[direction:direction + 1, :]
    softplus = jnp.maximum(nl, 0.0) + jnp.log1p(jnp.exp(-jnp.abs(nl)))
    log_a = (-LRU_C) * r * softplus
    a = jnp.exp(log_a)
    return a, jnp.sqrt(-jnp.tanh(log_a) * (a * a + 1.0)) * (ig * u)


def _scan_tile(a, bb, carry, reverse, sa_ref, sb_ref, sc_ref, out_ref):
    t, w = a.shape
    g = t // HALO
    a3 = a.reshape(g, HALO, w)
    b3 = bb.reshape(g, HALO, w)
    sub = lax.broadcasted_iota(jnp.int32, (g, HALO, w), 1)
    d = 1
    while d < HALO:
        ok = (sub < HALO - d) if reverse else (sub >= d)
        shift = HALO - d if reverse else d
        a_r = pltpu.roll(a3, shift, 1)
        b_r = pltpu.roll(b3, shift, 1)
        b3 = jnp.where(ok, a3 * b_r + b3, b3)
        a3 = jnp.where(ok, a3 * a_r, a3)
        d *= 2
    sa_ref[...] = a3.reshape(t, w)
    sb_ref[...] = b3.reshape(t, w)
    edge = 0 if reverse else HALO - 1
    at = sa_ref[pl.ds(edge, g, stride=HALO), :]
    bt = sb_ref[pl.ds(edge, g, stride=HALO), :]
    d = 1
    while d < g:
        a_s = _shift_rows(at, d, 1.0, reverse)
        b_s = _shift_rows(bt, d, 0.0, reverse)
        bt = at * b_s + bt
        at = at * a_s
        d *= 2
    h_end = at * carry + bt
    sc_ref[...] = _shift_rows(h_end, 1, carry, reverse)
    for gi in range(g):
        rows = slice(gi * HALO, (gi + 1) * HALO)
        out_ref[rows, :] = sa_ref[rows, :] * sc_ref[gi:gi + 1, :] + sb_ref[rows, :]
    return h_end[0:1, :] if reverse else h_end[g - 1:g, :]


def _scan_kernel(fp_ref, fc_ref, fn_ref, rp_ref, rc_ref, rn_ref, h0_ref, cw_ref, cb_ref,
                 wa_ref, ba_ref, wx_ref, bx_ref, lam_ref, hf_ref, hb_ref,
                 carry_ref, sa_ref, sb_ref, sc_ref):
    i = pl.program_id(1)
    nt = pl.num_programs(1)

    @pl.when(i == 0)
    def _():
        carry_ref[...] = h0_ref[...]

    par = (cw_ref, cb_ref, wa_ref, ba_ref, wx_ref, bx_ref, lam_ref)
    tiles = ((fp_ref, fc_ref, fn_ref, i > 0, i < nt - 1, hf_ref),
             (rp_ref, rc_ref, rn_ref, i < nt - 1, i > 0, hb_ref))
    for direction, (p_ref, c_ref, n_ref, has_prev, has_next, out_ref) in enumerate(tiles):
        a, bb = _lru_coeffs(p_ref, c_ref, n_ref, has_prev, has_next, *par, direction)
        for j in range(a.shape[1] // LANES):
            cols = slice(j * LANES, (j + 1) * LANES)
            carry_ref[direction:direction + 1, cols] = _scan_tile(
                a[:, cols], bb[:, cols], carry_ref[direction:direction + 1, cols], direction == 1,
                sa_ref.at[direction, j], sb_ref.at[direction, j], sc_ref.at[direction, j],
                out_ref.at[:, cols])


def _halo_specs(tile, width, n_rows, tile_index):
    r = tile // HALO
    last = n_rows // HALO - 1
    prev = pl.BlockSpec((None, HALO, width),
                        lambda bi, i: (bi, jnp.maximum(tile_index(i) * r - 1, 0), 0))
    cur = pl.BlockSpec((None, tile, width), lambda bi, i: (bi, tile_index(i), 0))
    nxt = pl.BlockSpec((None, HALO, width),
                       lambda bi, i: (bi, jnp.minimum((tile_index(i) + 1) * r, last), 0))
    return [prev, cur, nxt]


def _scan(cx, h0, cw, cb, wa, ba, wx, bx, lam, layer, tile):
    b, s, w = cx.shape
    nt = s // tile
    fwd = lambda i: i
    rev = lambda i: nt - 1 - i
    return pl.pallas_call(
        _scan_kernel,
        grid=(b, nt),
        in_specs=(_halo_specs(tile, w, s, fwd) + _halo_specs(tile, w, s, rev)
                  + [pl.BlockSpec((None, 2, w), lambda bi, i: (bi, 0, 0)),
                     _slab((LRU_CONV_K, w), layer), _slab((1, w), layer),
                     _slab((2, w, w), layer), _slab((2, w), layer),
                     _slab((2, w, w), layer), _slab((2, w), layer), _slab((2, w), layer)]),
        out_specs=[pl.BlockSpec((None, tile, w), lambda bi, i: (bi, i, 0)),
                   pl.BlockSpec((None, tile, w), lambda bi, i: (bi, nt - 1 - i, 0))],
        out_shape=[jax.ShapeDtypeStruct((b, s, w), F32)] * 2,
        scratch_shapes=[pltpu.VMEM((2, w), F32),
                        pltpu.VMEM((2, w // LANES, tile, LANES), F32),
                        pltpu.VMEM((2, w // LANES, tile, LANES), F32),
                        pltpu.VMEM((2, w // LANES, tile // HALO, LANES), F32)],
        compiler_params=_params(2),
        name="lru_scan",
    )(cx, cx, cx, cx, cx, cx, h0, cw, cb, wa, ba, wx, bx, lam)


def _softmax_pv(parts, sink):
    m = None
    for s, _ in parts:
        mx = jnp.max(s, axis=-1, keepdims=True)
        m = mx if m is None else jnp.maximum(m, mx)
    if sink is not None:
        m = jnp.maximum(m, sink)
    den = None
    out = None
    for s, v in parts:
        p = jnp.exp2(s - m)
        sm = jnp.sum(p, axis=-1, keepdims=True)
        den = sm if den is None else den + sm
        pv = _dot(p.astype(BF16), v)
        out = pv if out is None else out + pv
    if sink is not None:
        den = den + jnp.exp2(sink - m)
    return out / den


def _dot_tn(a, b):
    return lax.dot_general(a, b, (((0,), (0,)), ((), ())), preferred_element_type=F32)


def _half_mask(shape, lower):
    lane = lax.broadcasted_iota(jnp.int32, shape, 1) % LANES
    return (lane < HEAD_DIM) if lower else (lane >= HEAD_DIM)


def _softmax_pv_t(parts, sink_row, lower):
    m = None
    for s, _ in parts:
        mx = jnp.max(s, axis=0, keepdims=True)
        m = mx if m is None else jnp.maximum(m, mx)
    if sink_row is not None:
        m = jnp.maximum(m, sink_row)
    res = None
    for s, v in parts:
        r = _dot_tn(v, jnp.exp2(s - m).astype(BF16))
        res = r if res is None else res + r
    out, den = (res[:HEAD_DIM], res[HEAD_DIM:HEAD_DIM + 1]) if lower else (res[HEAD_DIM:], res[0:1])
    if sink_row is not None:
        den = den + jnp.exp2(sink_row - m)
    return out * (1.0 / den)


WIN_Q_COLS = (0, 2, 1, 3)


def _window_kernel(qr_ref, qp_ref, kp_ref, kc_ref, kn_ref, vp_ref, vc_ref, vn_ref,
                   kx_ref, vx_ref, sink_ref, band_ref, o_ref):
    i = pl.program_id(1)
    nt = pl.num_programs(1)
    tq = qr_ref.shape[0]
    nblk = tq // WINDOW
    qr = qr_ref[...]
    qp = qp_ref[...]
    kcat = jnp.concatenate([kp_ref[...], kc_ref[...], kn_ref[...]], axis=0)
    vcat = jnp.concatenate([vp_ref[...], vc_ref[...], vn_ref[...]], axis=0)
    kx = kx_ref[...]
    vx = vx_ref[...]
    for hk in range(2):
        lower = hk == 0
        qm = _half_mask((tq, LANES), lower)
        zero = jnp.zeros((tq, LANES), BF16)
        qra, qrb = jnp.where(qm, qr[:, :LANES], zero), jnp.where(qm, qr[:, LANES:], zero)
        qpa, qpb = jnp.where(qm, qp[:, :LANES], zero), jnp.where(qm, qp[:, LANES:], zero)
        vw = jnp.where(_half_mask(vcat.shape, lower), vcat, jnp.ones_like(vcat))
        vxw = jnp.where(_half_mask(vx.shape, lower), vx, jnp.ones_like(vx))
        s_ctx = _dot_nt(kx, jnp.concatenate([qpa, qpb], axis=0))
        for j in range(nblk):
            rq = slice(j * WINDOW, (j + 1) * WINDOW)
            rk = slice(j * WINDOW, (j + 3) * WINDOW)
            if j == 0:
                band = band_ref[jnp.where(i == 0, 1, 0)]
            elif j == nblk - 1:
                band = band_ref[jnp.where(i == nt - 1, 2, 0)]
            else:
                band = band_ref[0]
            s_loc = (_dot_nt(kcat[rk], jnp.concatenate([qra[rq], qrb[rq]], axis=0))
                     + jnp.concatenate([band, band], axis=1))
            s_cx = jnp.concatenate([s_ctx[:, rq], s_ctx[:, tq + j * WINDOW:tq + (j + 1) * WINDOW]], axis=1)
            o = _softmax_pv_t([(s_loc, vw[rk]), (s_cx, vxw)], sink_ref[hk], lower).astype(o_ref.dtype)
            ha, hb = 2 * hk, 2 * hk + 1
            o_ref[ha * HEAD_DIM:(ha + 1) * HEAD_DIM, rq] = o[:, :WINDOW]
            o_ref[hb * HEAD_DIM:(hb + 1) * HEAD_DIM, rq] = o[:, WINDOW:]


def _window_band():
    kk = np.arange(3 * WINDOW)[:, None]
    r = np.arange(WINDOW)[None, :]
    ok = np.abs(kk - WINDOW - r) <= WINDOW
    variants = [ok, ok & (kk >= WINDOW), ok & (kk < 2 * WINDOW)]
    return jnp.asarray(np.where(np.stack(variants), 0.0, NEG_INF).astype(np.float32))


def _window_attention(qr, qp, k, v, kx, vx, sink_rows, layer, tile):
    b, s, qw = qr.shape
    kw = k.shape[2]
    lx = kx.shape[1]
    assert kw == LANES and qw == 2 * LANES and tile % WINDOW == 0 and s >= 2 * WINDOW
    r = tile // WINDOW
    last = s // WINDOW - 1
    band = _window_band()
    qspec = pl.BlockSpec((None, tile, qw), lambda bi, i: (bi, i, 0))
    prev = pl.BlockSpec((None, WINDOW, kw), lambda bi, i: (bi, jnp.maximum(i * r - 1, 0), 0))
    cur = pl.BlockSpec((None, tile, kw), lambda bi, i: (bi, i, 0))
    nxt = pl.BlockSpec((None, WINDOW, kw), lambda bi, i: (bi, jnp.minimum((i + 1) * r, last), 0))
    ctx = pl.BlockSpec((None, lx, kw), lambda bi, i: (bi, 0, 0))
    return pl.pallas_call(
        _window_kernel,
        grid=(b, s // tile),
        in_specs=[qspec, qspec, prev, cur, nxt, prev, cur, nxt, ctx, ctx,
                  _slab(sink_rows.shape[1:], layer), _full(band.shape)],
        out_specs=pl.BlockSpec((None, qw, tile), lambda bi, i: (bi, 0, i)),
        out_shape=jax.ShapeDtypeStruct((b, qw, s), BF16),
        compiler_params=_params(2),
        name="window_attention",
    )(qr, qp, k, k, k, v, v, v, kx, vx, sink_rows, band)


def _nbr_kernel(q_ref, kp_ref, kc_ref, kn_ref, vp_ref, vc_ref, vn_ref, kx_ref, vx_ref, bias_ref, o_ref):
    q = q_ref[...]
    tq = q.shape[0]
    kcat = jnp.concatenate([kp_ref[...], kc_ref[...], kn_ref[...]], axis=0)
    vcat = jnp.concatenate([vp_ref[...], vc_ref[...], vn_ref[...]], axis=0)
    kx = kx_ref[...]
    vx = vx_ref[...]
    for g in range(q.shape[1] // LANES):
        cols = slice(g * LANES, (g + 1) * LANES)
        qg, vg, vxg = q[:, cols], vcat[:, cols], vx[:, cols]
        low = _half_mask(qg.shape, True)
        zero = jnp.zeros_like(qg)
        qcat = jnp.concatenate([jnp.where(low, qg, zero), jnp.where(low, zero, qg)], axis=0)
        s_loc = _dot_nt(kcat[:, cols], qcat) + bias_ref[g]
        s_ctx = _dot_nt(kx[:, cols], qcat)
        for e in range(2):
            lower = e == 0
            qs = slice(e * tq, (e + 1) * tq)
            vw = jnp.where(_half_mask(vg.shape, lower), vg, jnp.ones_like(vg))
            vxw = jnp.where(_half_mask(vxg.shape, lower), vxg, jnp.ones_like(vxg))
            o = _softmax_pv_t([(s_loc[:, qs], vw), (s_ctx[:, qs], vxw)], None, lower)
            h = 2 * g + e
            o_ref[h * HEAD_DIM:(h + 1) * HEAD_DIM, :] = o.astype(o_ref.dtype)


def _nbr_bias(rel_bias, rows, tile_rows):
    n_tiles = rows // tile_rows
    kr = NA_ROWS
    qc = np.arange(GRID_W)
    kc = np.arange(GRID_W)
    dc = np.clip(kc[None, :] - qc[:, None], 1 - NA_COLS, NA_COLS - 1) + NA_COLS - 1
    c_start = np.clip(qc - NA_COLS // 2, 0, GRID_W - NA_COLS)
    col_ok = (kc[None, :] >= c_start[:, None]) & (kc[None, :] < c_start[:, None] + NA_COLS)
    dr_all, ok_all = [], []
    for ti in (0, 1, n_tiles - 1):
        r = ti * tile_rows + np.arange(tile_rows)
        start = np.clip(r - kr // 2, 0, rows - kr)
        blk = np.arange(3)[:, None] - 1 + ti
        krow = blk * tile_rows + np.arange(tile_rows)[None, :]
        blk_ok = (blk >= 0) & (blk < n_tiles)
        ok_all.append((krow[None] >= start[:, None, None]) & (krow[None] < start[:, None, None] + kr)
                      & blk_ok[None])
        dr_all.append(np.clip(krow[None] - r[:, None, None] + NA_ROWS - 1, 0, 2 * NA_ROWS - 2))
    oh_dc = jnp.asarray(np.eye(2 * NA_COLS - 1, dtype=np.float32)[dc])
    oh_dr = jnp.asarray(np.eye(2 * NA_ROWS - 1, dtype=np.float32)[np.stack(dr_all)])
    hi = lax.Precision.HIGHEST
    n_l, n_h = rel_bias.shape[:2]
    rb = rel_bias.astype(F32).reshape(n_l, n_h // 2, 2, 2 * NA_ROWS - 1, 2 * NA_COLS - 1)
    toe = jnp.einsum('qkc,lgerc->lgerqk', oh_dc, rb, precision=hi)
    bias = jnp.einsum('vpbjr,lgerqk->lvgbjkepq', oh_dr, toe, precision=hi)
    ok = (jnp.asarray(np.stack(ok_all).transpose(0, 2, 3, 1))[:, None, :, :, None, None, :, None]
          & jnp.asarray(col_ok.T)[None, None, None, None, :, None, None, :])
    bias = jnp.where(ok[None], bias, NEG_INF)
    t = tile_rows * GRID_W
    return bias.reshape(n_l, 3, n_h // 2, 3 * t, 2 * t)


def _nbr_attention(q, k, v, kx, vx, bias, layer, tile):
    b, s, w = q.shape
    lx = kx.shape[1]
    nt = s // tile
    tok = pl.BlockSpec((None, tile, w), lambda bi, i: (bi, i, 0))
    prev = pl.BlockSpec((None, tile, w), lambda bi, i: (bi, jnp.maximum(i - 1, 0), 0))
    nxt = pl.BlockSpec((None, tile, w), lambda bi, i: (bi, jnp.minimum(i + 1, nt - 1), 0))
    ctx = pl.BlockSpec((None, lx, w), lambda bi, i: (bi, 0, 0))
    variant = lambda bi, i: (layer, jnp.where(i == 0, 0, jnp.where(i == nt - 1, 2, 1)), 0, 0, 0)
    bspec = pl.BlockSpec((None, None) + bias.shape[2:], variant)
    return pl.pallas_call(
        _nbr_kernel,
        grid=(b, nt),
        in_specs=[tok, prev, tok, nxt, prev, tok, nxt, ctx, ctx, bspec],
        out_specs=pl.BlockSpec((None, w, tile), lambda bi, i: (bi, 0, i)),
        out_shape=jax.ShapeDtypeStruct((b, w, s), BF16),
        compiler_params=_params(2),
        name="nbr_attention",
    )(q, k, k, k, v, v, v, kx, vx, bias)


def _ctx_attn_kernel(q_ref, k_ref, v_ref, sink_ref, o_ref, *, n_kv, group, use_sink, q_cols):
    q = q_ref[...]
    k = k_ref[...]
    v = v_ref[...]
    l = q.shape[0]
    outs = []
    for hk in range(n_kv):
        ks = slice(hk * HEAD_DIM, (hk + 1) * HEAD_DIM)
        for h in range(hk * group, (hk + 1) * group):
            s = _dot_nt(q[:, q_cols[h] * HEAD_DIM:(q_cols[h] + 1) * HEAD_DIM], k[:, ks])
            sink = jnp.broadcast_to(sink_ref[0:1, h:h + 1], (l, 1)) if use_sink else None
            outs.append(_softmax_pv([(s, v[:, ks])], sink))
    o_ref[...] = jnp.concatenate(outs, axis=1).astype(o_ref.dtype)


def _ctx_attention(q, k, v, sink, layer, use_sink, q_cols):
    b, l, qw = q.shape
    kw = k.shape[2]
    n_kv = kw // HEAD_DIM
    group = (qw // HEAD_DIM) // n_kv
    return pl.pallas_call(
        functools.partial(_ctx_attn_kernel, n_kv=n_kv, group=group, use_sink=use_sink, q_cols=q_cols),
        grid=(b,),
        in_specs=[pl.BlockSpec((None, l, qw), lambda bi: (bi, 0, 0)),
                  pl.BlockSpec((None, l, kw), lambda bi: (bi, 0, 0)),
                  pl.BlockSpec((None, l, kw), lambda bi: (bi, 0, 0)),
                  _slab(sink.shape[1:], layer)],
        out_specs=pl.BlockSpec((None, l, qw), lambda bi: (bi, 0, 0)),
        out_shape=jax.ShapeDtypeStruct((b, l, qw), BF16),
        compiler_params=_params(1),
        name="ctx_attention",
    )(q, k, v, sink)


def _merge_kernel(x_ref, mod_ref, g_ref, pp_ref, pc_ref, pn_ref, ab_ref, caw_ref,
                  yb_ref, hf_ref, hb_ref, cg_ref, yd_ref,
                  wgl_ref, bg_ref, wbr_ref, wo_ref, o_ref):
    i = pl.program_id(1)
    nt = pl.num_programs(1)
    x = x_ref[...]
    t, d = x.shape
    h = _adaln(x, g_ref[...], mod_ref[3:4, :], mod_ref[4:5, :]).astype(BF16)
    pc = pc_ref[...]
    prev = jnp.where(i > 0, pp_ref[...], 0.0)
    nxt = jnp.where(i < nt - 1, pn_ref[...], 0.0)
    ext = jnp.concatenate([prev, pc, nxt], axis=0)
    conv = (ext[HALO - 1:HALO - 1 + t] * caw_ref[0:1, :] + pc * caw_ref[1:2, :]
            + ext[HALO + 1:HALO + 1 + t] * caw_ref[2:3, :])
    y_a = (ab_ref[...] * conv).astype(BF16)
    y_c = ((hf_ref[...] + hb_ref[...]) * _gelu_tanh(cg_ref[...])).astype(BF16)
    ys = ((y_a, _dot), (yb_ref[...], _dot_tn), (y_c, _dot), (yd_ref[...], _dot_tn))
    merged = None
    for n, (y, dot) in enumerate(ys):
        gate = _sigmoid(_dot(h, wgl_ref[:, n * d:(n + 1) * d]) + bg_ref[0:1, n * d:(n + 1) * d])
        term = gate * dot(y, wbr_ref[n])
        merged = term if merged is None else merged + term
    out = _dot(merged.astype(BF16), wo_ref[...])
    o_ref[...] = x + mod_ref[5:6, :] * out


def _merge(x, mods, norm_g, pa, ab, caw, yb, hf, hb, cg, yd, wgl, bg, wbr, wo, layer, ctx_row, tile):
    b, s, d = x.shape
    mw = pa.shape[2]
    tok = lambda width: pl.BlockSpec((None, tile, width), lambda bi, i: (bi, i, 0))
    tok_t = pl.BlockSpec((None, mw, tile), lambda bi, i: (bi, 0, i))
    return pl.pallas_call(
        _merge_kernel,
        grid=(b, s // tile),
        in_specs=([tok(d), _mod_spec(d, layer, ctx_row), _slab((1, d), layer, 1)]
                  + _halo_specs(tile, mw, s, lambda i: i)
                  + [tok(mw), _slab((CONV_K, mw), layer),
                     tok_t, tok(mw), tok(mw), tok(mw), tok_t,
                     _slab(wgl.shape[1:], layer), _slab(bg.shape[1:], layer),
                     _slab(wbr.shape[1:], layer), _slab(wo.shape[1:], layer)]),
        out_specs=tok(d),
        out_shape=jax.ShapeDtypeStruct((b, s, d), F32),
        compiler_params=_params(2),
        name="merge",
    )(x, mods, norm_g, pa, pa, pa, ab, caw, yb, hf, hb, cg, yd, wgl, bg, wbr, wo)


def _rope_tables(n_tok, width):
    half = HEAD_DIM // 2
    nf = half // 2
    inv_freq = ROPE_BASE ** (-jnp.arange(nf, dtype=F32) / nf)
    pos = jnp.arange(n_tok)
    ang_r = (pos // GRID_W).astype(F32)[:, None] * inv_freq[None, :]
    ang_c = (pos % GRID_W).astype(F32)[:, None] * inv_freq[None, :]
    cos = jnp.concatenate([jnp.cos(ang_r)] * 2 + [jnp.cos(ang_c)] * 2, axis=1)
    sin = jnp.concatenate([-jnp.sin(ang_r), jnp.sin(ang_r), -jnp.sin(ang_c), jnp.sin(ang_c)], axis=1)
    reps = width // HEAD_DIM
    return jnp.tile(cos, (1, reps)), jnp.tile(sin, (1, reps))


def _block_diag(w):
    *lead, n, c, e = w.shape
    eye = jnp.eye(n, dtype=w.dtype)
    return (eye[:, None, :, None] * w[..., :, :, None, :]).reshape(*lead, n * c, n * e)


def _pick_tile(n, target):
    t = min(n, target)
    while n % t:
        t //= 2
    return t


def kernel(x, c, ctx, c_ctx, w_mod, b_mod, norm_g, ffn_w_gate, ffn_w_up, ffn_w_down, w_in, b_gate,
           conv_a_w, qk_norm_g, attn_sink, lru_conv_w, lru_conv_b, lru_w_a, lru_b_a, lru_w_x, lru_b_x,
           lru_lam, na_rel_bias, w_branch, w_out):
    b, s, d = x.shape
    lx = ctx.shape[1]
    depth = w_mod.shape[0]
    mw = d // 4
    kvw = mw // 2
    n_mix = w_in.shape[2] - N_BRANCH * d
    rows = s // GRID_W
    assert s % GRID_W == 0 and rows >= 4 * NA_ROWS and lx % HALO == 0

    tile = _pick_tile(s, 512)
    tile_x = _pick_tile(lx, 512)
    tile_win = _pick_tile(s, 512)
    nbr_rows = 4
    tile_nbr = nbr_rows * GRID_W

    cs = jnp.zeros((8, d), F32).at[:b].set(c).at[b].set(c_ctx)
    mods = _modulation(cs, w_mod, b_mod).reshape(depth, 8, N_MOD, d)
    lat, cx_row = None, b

    cos, sin = _rope_tables(s, kvw)
    cos_x, sin_x = jnp.ones((lx, kvw), F32), jnp.zeros((lx, kvw), F32)
    lane = np.arange(mw)
    mhead = jnp.asarray((lane[:, None] // HEAD_DIM == lane[None, :] // HEAD_DIM) / HEAD_DIM, dtype=BF16)

    ng = norm_g.reshape(depth, 3, 1, d)
    wg, wu, wd = ffn_w_gate.astype(BF16), ffn_w_up.astype(BF16), ffn_w_down.astype(BF16)
    q0 = 3 * mw
    head_at_col = np.argsort(np.asarray(WIN_Q_COLS))
    w_mix = jnp.concatenate(
        [w_in[:, :, :q0]]
        + [w_in[:, :, q0 + h * HEAD_DIM:q0 + (h + 1) * HEAD_DIM] for h in head_at_col]
        + [w_in[:, :, q0 + mw:n_mix]], axis=2).astype(BF16)
    w_gl = w_in[:, :, n_mix:].astype(BF16)
    qkg = jnp.tile(qk_norm_g, (1, 1, mw // HEAD_DIM))
    sink = (attn_sink * LOG2E).reshape(depth, 1, -1)
    sink_rows = jnp.repeat(sink.reshape(depth, 2, 1, 2), WINDOW, axis=3)
    lru_par = (lru_conv_w, lru_conv_b.reshape(depth, 1, mw), _block_diag(lru_w_a).astype(BF16), lru_b_a,
               _block_diag(lru_w_x).astype(BF16), lru_b_x, lru_lam)
    bias = _nbr_bias(na_rel_bias * LOG2E, rows, nbr_rows)
    bg = b_gate.reshape(depth, 1, -1)
    wbr = w_branch.astype(BF16)
    wo = w_out.astype(BF16)

    xc = ctx
    for l in range(depth):
        ctx_out = l < depth - 1
        x = _ffn(x, mods, ng, wg, wu, wd, l, 0, lat, tile)
        xc = _ffn(xc, mods, ng, wg, wu, wd, l, 0, cx_row, tile_x)
        (pa, ab, qr, qp, kr, vb, cxl, cg, dq, dk, dv) = _inproj(
            x, mods, ng, w_mix, qkg, mhead, cos, sin, l, lat, tile)
        (pa_c, ab_c, _, qp_c, k_c, vb_c, cx_c, cg_c, dq_c, dk_c, dv_c) = _inproj(
            xc, mods, ng, w_mix, qkg, mhead, cos_x, sin_x, l, cx_row, tile_x)

        hf_c, hb_c = _scan(cx_c, jnp.zeros((b, 2, mw), F32), *lru_par, l, tile_x)
        h0 = jnp.stack([hf_c[:, lx - 1], hb_c[:, 0]], axis=1)
        hf, hb = _scan(cxl, h0, *lru_par, l, tile)
        yb = _window_attention(qr, qp, kr, vb, k_c, vb_c, sink_rows, l, tile_win)
        yd = _nbr_attention(dq, dk, dv, dk_c, dv_c, bias, l, tile_nbr)
        x = _merge(x, mods, ng, pa, ab, conv_a_w, yb, hf, hb, cg, yd, w_gl, bg, wbr, wo, l, lat, tile)
        x = _ffn(x, mods, ng, wg, wu, wd, l, 1, lat, tile)
        if ctx_out:
            yb_c = jnp.swapaxes(_ctx_attention(qp_c, k_c, vb_c, sink, l, True, WIN_Q_COLS), 1, 2)
            yd_c = jnp.swapaxes(_ctx_attention(dq_c, dk_c, dv_c, sink, l, False, (0, 1, 2, 3)), 1, 2)
            xc = _merge(xc, mods, ng, pa_c, ab_c, conv_a_w, yb_c, hf_c, hb_c, cg_c, yd_c,
                        w_gl, bg, wbr, wo, l, cx_row, tile_x)
            xc = _ffn(xc, mods, ng, wg, wu, wd, l, 1, cx_row, tile_x)
    return x
```

```python
import functools

import jax
import jax.numpy as jnp
import numpy as np
from jax import lax
from jax.experimental import pallas as pl
from jax.experimental.pallas import tpu as pltpu

HEAD_DIM = 64
GRID_W = 64
WINDOW = 128
NA_ROWS = 8
NA_COLS = 16
N_BRANCH = 4
CONV_K = 3
LRU_CONV_K = 4
LRU_C = 8.0
ROPE_BASE = 10000.0
EPS = 1e-6
NEG_INF = -1e30
N_MOD = 9
LOG2E = 1.4426950408889634

V7X_VMEM_LIMIT_BYTES = 60 * 1024 * 1024
HALO = 8
LANES = 128
MERGE_CHUNK = 512

F32 = jnp.float32
BF16 = jnp.bfloat16


def _params(n_grid):
    return pltpu.CompilerParams(
        dimension_semantics=("arbitrary",) * n_grid,
        vmem_limit_bytes=V7X_VMEM_LIMIT_BYTES)


def _dot(a, b):
    return jnp.dot(a, b, preferred_element_type=F32)


def _dot_nt(a, b):
    return lax.dot_general(a, b, (((1,), (1,)), ((), ())), preferred_element_type=F32)


def _sigmoid(x):
    return jax.nn.sigmoid(x)


def _gelu_tanh(x):
    return 0.5 * x * (1.0 + jnp.tanh(0.7978845608028654 * (x + 0.044715 * (x * x * x))))


def _adaln(x, g, shift, scale):
    ms = jnp.mean(x * x, axis=-1, keepdims=True)
    return (x * lax.rsqrt(ms + EPS) * g) * (1.0 + scale) + shift


def _full(shape):
    n = len(shape)
    return pl.BlockSpec(shape, lambda *_: (0,) * n)


def _slab(shape, *lead):
    n = len(shape)
    return pl.BlockSpec((None,) * len(lead) + tuple(shape), lambda *_: tuple(lead) + (0,) * n,
                        pipeline_mode=pl.Buffered(1))


def _mod_spec(d, layer, ctx_row):
    if ctx_row is None:
        return pl.BlockSpec((None, None, N_MOD, d), lambda bi, i: (layer, bi, 0, 0))
    return pl.BlockSpec((None, None, N_MOD, d), lambda bi, i: (layer, ctx_row, 0, 0))


def _mod_kernel(c_ref, w_ref, b_ref, o_ref):
    cs = c_ref[...]
    s = cs * _sigmoid(cs)
    o_ref[...] = jnp.dot(s, w_ref[...], preferred_element_type=F32,
                         precision=lax.Precision.HIGHEST) + b_ref[...]


def _modulation(cs, w_mod, b_mod):
    depth, d, n = w_mod.shape
    tn = n // 8
    return pl.pallas_call(
        _mod_kernel,
        grid=(depth, n // tn),
        in_specs=[pl.BlockSpec((8, d), lambda l, j: (0, 0)),
                  pl.BlockSpec((None, d, tn), lambda l, j: (l, 0, j)),
                  pl.BlockSpec((None, 1, tn), lambda l, j: (l, 0, j))],
        out_specs=pl.BlockSpec((None, 8, tn), lambda l, j: (l, 0, j)),
        out_shape=jax.ShapeDtypeStruct((depth, 8, n), F32),
        compiler_params=_params(2),
        name="modulation",
    )(cs, w_mod, b_mod.reshape(depth, 1, n))


def _ffn_kernel(x_ref, mod_ref, g_ref, wg_ref, wu_ref, wd_ref, o_ref, *, row0, chunks, sub):
    for r0 in range(0, x_ref.shape[0], sub):
        rows = slice(r0, r0 + sub)
        x = x_ref[rows, :]
        h = _adaln(x, g_ref[...], mod_ref[row0:row0 + 1, :], mod_ref[row0 + 1:row0 + 2, :]).astype(BF16)
        acc = None
        for (lo, hi) in chunks:
            a = _dot(h, wg_ref[:, lo:hi])
            u = _dot(h, wu_ref[:, lo:hi])
            act = (a * _sigmoid(a) * u).astype(BF16)
            part = _dot(act, wd_ref[lo:hi, :])
            acc = part if acc is None else acc + part
        o_ref[rows, :] = x + (0.5 * mod_ref[row0 + 2:row0 + 3, :]) * acc


def _ffn(x, mods, norm_g, wg, wu, wd, layer, which, ctx_row, tile, sub):
    b, s, d = x.shape
    f = wg.shape[3]
    half = (f // 2 + 255) // 256 * 256
    chunks = ((0, half), (half, f)) if half < f else ((0, f),)
    return pl.pallas_call(
        functools.partial(_ffn_kernel, row0=6 * which, chunks=chunks, sub=min(sub, tile)),
        grid=(b, s // tile),
        in_specs=[pl.BlockSpec((None, tile, d), lambda bi, i: (bi, i, 0)),
                  _mod_spec(d, layer, ctx_row),
                  _slab((1, d), layer, 2 * which),
                  _slab((d, f), layer, which), _slab((d, f), layer, which), _slab((f, d), layer, which)],
        out_specs=pl.BlockSpec((None, tile, d), lambda bi, i: (bi, i, 0)),
        out_shape=jax.ShapeDtypeStruct((b, s, d), F32),
        compiler_params=_params(2),
        name="ffn",
    )(x, mods, norm_g, wg, wu, wd)


def _head_rms(t, m_ref, g):
    n = t.shape[1]
    sq = t * t
    hi = sq.astype(BF16)
    lo = (sq - hi.astype(F32)).astype(BF16)
    m = m_ref[0:n, 0:n]
    ms = _dot(hi, m) + _dot(lo, m)
    return t * lax.rsqrt(ms + EPS) * g


def _rope(t, cos, sin_signed):
    n = t.shape[1]
    lane = lax.broadcasted_iota(jnp.int32, t.shape, 1)
    partner = jnp.where((lane % 32) < 16, pltpu.roll(t, n - 16, 1), pltpu.roll(t, 16, 1))
    return t * cos + partner * sin_signed


def _inproj_kernel(x_ref, mod_ref, g_ref, w_ref, qkg_ref, m_ref, cos_ref, sin_ref,
                   pa_ref, ab_ref, qr_ref, qp_ref, kr_ref, vb_ref, cx_ref, cg_ref,
                   dq_ref, dk_ref, dv_ref, *, mw, sub):
    kvw = mw // 2
    scale = LOG2E * HEAD_DIM ** -0.5
    for r0 in range(0, x_ref.shape[0], sub):
        rows = slice(r0, r0 + sub)
        h = _adaln(x_ref[rows, :], g_ref[...], mod_ref[3:4, :], mod_ref[4:5, :]).astype(BF16)
        u = _dot(h, w_ref[...])
        o = 0
        ax = u[:, o:o + mw]; o += mw
        ab = u[:, o:o + mw]; o += mw
        ac = u[:, o:o + mw]; o += mw
        bq = u[:, o:o + mw]; o += mw
        qa, qb = bq[:, :LANES], bq[:, LANES:]
        low = _half_mask(qa.shape, True)
        bq = jnp.concatenate([jnp.where(low, qa, pltpu.roll(qb, HEAD_DIM, 1)),
                              jnp.where(low, pltpu.roll(qa, HEAD_DIM, 1), qb)], axis=1)
        bk = u[:, o:o + kvw]; o += kvw
        bv = u[:, o:o + kvw]; o += kvw
        cx = u[:, o:o + mw]; o += mw
        cg = u[:, o:o + mw]; o += mw
        dq = u[:, o:o + mw]; o += mw
        dk = u[:, o:o + mw]; o += mw
        dv = u[:, o:o + mw]; o += mw
        pa_ref[rows, :] = ac * ax
        ab_ref[rows, :] = ab
        cx_ref[rows, :] = cx
        cg_ref[rows, :] = cg
        cos = cos_ref[rows, :]
        sin = sin_ref[rows, :]
        cos2 = jnp.concatenate([cos, cos], axis=1)
        sin2 = jnp.concatenate([sin, sin], axis=1)
        q = _head_rms(bq, m_ref, qkg_ref[0:1, :]) * scale
        qp_ref[rows, :] = q.astype(BF16)
        qr_ref[rows, :] = _rope(q, cos2, sin2).astype(BF16)
        k = _head_rms(bk, m_ref, qkg_ref[1:2, 0:kvw])
        kr_ref[rows, :] = _rope(k, cos, sin).astype(BF16)
        vb_ref[rows, :] = bv.astype(BF16)
        dq_ref[rows, :] = (_head_rms(dq, m_ref, qkg_ref[2:3, :]) * scale).astype(BF16)
        dk_ref[rows, :] = _head_rms(dk, m_ref, qkg_ref[3:4, :]).astype(BF16)
        dv_ref[rows, :] = dv.astype(BF16)


def _inproj(x, mods, norm_g, w_in, qkg, mhead, cos, sin, layer, ctx_row, tile, sub):
    b, s, d = x.shape
    mw = qkg.shape[2]
    kvw = mw // 2
    nw = 9 * mw + 2 * kvw
    assert mw == 2 * LANES
    tok = lambda width: pl.BlockSpec((None, tile, width), lambda bi, i: (bi, i, 0))
    widths = (mw, mw, mw, mw, kvw, kvw, mw, mw, mw, mw, mw)
    dtypes = (F32, F32, BF16, BF16, BF16, BF16, F32, F32, BF16, BF16, BF16)
    return pl.pallas_call(
        functools.partial(_inproj_kernel, mw=mw, sub=min(sub, tile)),
        grid=(b, s // tile),
        in_specs=[tok(d),
                  _mod_spec(d, layer, ctx_row),
                  _slab((1, d), layer, 1), _slab((d, nw), layer), _slab((4, mw), layer), _full((mw, mw)),
                  pl.BlockSpec((tile, kvw), lambda bi, i: (i, 0)),
                  pl.BlockSpec((tile, kvw), lambda bi, i: (i, 0))],
        out_specs=[tok(wd) for wd in widths],
        out_shape=[jax.ShapeDtypeStruct((b, s, wd), dt) for wd, dt in zip(widths, dtypes)],
        compiler_params=_params(2),
        name="inproj",
    )(x, mods, norm_g, w_in, qkg, mhead, cos, sin)


def _shift_rows(t, d, fill, reverse):
    n = t.shape[0]
    row = lax.broadcasted_iota(jnp.int32, t.shape, 0)
    if reverse:
        return jnp.where(row < n - d, pltpu.roll(t, n - d, 0), fill)
    return jnp.where(row >= d, pltpu.roll(t, d, 0), fill)


def _lru_coeffs(prev_ref, cur_ref, next_ref, has_prev, has_next, cw_ref, cb_ref,
                wa_ref, ba_ref, wx_ref, bx_ref, lam_ref, direction):
    cur = cur_ref[...]
    t = cur.shape[0]
    prev = jnp.where(has_prev, prev_ref[...], 0.0)
    nxt = jnp.where(has_next, next_ref[...], 0.0)
    ext = jnp.concatenate([prev, cur, nxt], axis=0)
    u = cb_ref[...] + cur * cw_ref[2:3, :]
    u = u + ext[HALO - 2:HALO - 2 + t] * cw_ref[0:1, :]
    u = u + ext[HALO - 1:HALO - 1 + t] * cw_ref[1:2, :]
    u = u + ext[HALO + 1:HALO + 1 + t] * cw_ref[3:4, :]
    ub = u.astype(BF16)
    r = _sigmoid(_dot(ub, wa_ref[direction]) + ba_ref[direction:direction + 1, :])
    ig = _sigmoid(_dot(ub, wx_ref[direction]) + bx_ref[direction:direction + 1, :])
    nl = -lam_ref[direction:direction + 1, :]
    softplus = jnp.maximum(nl, 0.0) + jnp.log1p(jnp.exp(-jnp.abs(nl)))
    log_a = (-LRU_C) * r * softplus
    a = jnp.exp(log_a)
    return a, jnp.sqrt(-jnp.tanh(log_a) * (a * a + 1.0)) * (ig * u)


def _scan_tile(a, bb, carry, reverse, sa_ref, sb_ref, sc_ref, out_ref):
    t, w = a.shape
    g = t // HALO
    a3 = a.reshape(g, HALO, w)
    b3 = bb.reshape(g, HALO, w)
    sub = lax.broadcasted_iota(jnp.int32, (g, HALO, w), 1)
    d = 1
    while d < HALO:
        ok = (sub < HALO - d) if reverse else (sub >= d)
        shift = HALO - d if reverse else d
        a_r = pltpu.roll(a3, shift, 1)
        b_r = pltpu.roll(b3, shift, 1)
        b3 = jnp.where(ok, a3 * b_r + b3, b3)
        a3 = jnp.where(ok, a3 * a_r, a3)
        d *= 2
    sa_ref[...] = a3.reshape(t, w)
    sb_ref[...] = b3.reshape(t, w)
    edge = 0 if reverse else HALO - 1
    at = sa_ref[pl.ds(edge, g, stride=HALO), :]
    bt = sb_ref[pl.ds(edge, g, stride=HALO), :]
    d = 1
    while d < g:
        a_s = _shift_rows(at, d, 1.0, reverse)
        b_s = _shift_rows(bt, d, 0.0, reverse)
        bt = at * b_s + bt
        at = at * a_s
        d *= 2
    h_end = at * carry + bt
    sc_ref[...] = _shift_rows(h_end, 1, carry, reverse)
    for gi in range(g):
        rows = slice(gi * HALO, (gi + 1) * HALO)
        out_ref[rows, :] = sa_ref[rows, :] * sc_ref[gi:gi + 1, :] + sb_ref[rows, :]
    return h_end[0:1, :] if reverse else h_end[g - 1:g, :]


def _scan_kernel(fp_ref, fc_ref, fn_ref, rp_ref, rc_ref, rn_ref, h0_ref, cw_ref, cb_ref,
                 wa_ref, ba_ref, wx_ref, bx_ref, lam_ref, hf_ref, hb_ref,
                 carry_ref, sa_ref, sb_ref, sc_ref):
    i = pl.program_id(1)
    nt = pl.num_programs(1)

    @pl.when(i == 0)
    def _():
        carry_ref[...] = h0_ref[...]

    par = (cw_ref, cb_ref, wa_ref, ba_ref, wx_ref, bx_ref, lam_ref)
    tiles = ((fp_ref, fc_ref, fn_ref, i > 0, i < nt - 1, hf_ref),
             (rp_ref, rc_ref, rn_ref, i < nt - 1, i > 0, hb_ref))
    for direction, (p_ref, c_ref, n_ref, has_prev, has_next, out_ref) in enumerate(tiles):
        a, bb = _lru_coeffs(p_ref, c_ref, n_ref, has_prev, has_next, *par, direction)
        for j in range(a.shape[1] // LANES):
            cols = slice(j * LANES, (j + 1) * LANES)
            carry_ref[direction:direction + 1, cols] = _scan_tile(
                a[:, cols], bb[:, cols], carry_ref[direction:direction + 1, cols], direction == 1,
                sa_ref.at[direction, j], sb_ref.at[direction, j], sc_ref.at[direction, j],
                out_ref.at[:, cols])


def _halo_specs(tile, width, n_rows, tile_index):
    r = tile // HALO
    last = n_rows // HALO - 1
    prev = pl.BlockSpec((None, HALO, width),
                        lambda bi, i: (bi, jnp.maximum(tile_index(i) * r - 1, 0), 0))
    cur = pl.BlockSpec((None, tile, width), lambda bi, i: (bi, tile_index(i), 0))
    nxt = pl.BlockSpec((None, HALO, width),
                       lambda bi, i: (bi, jnp.minimum((tile_index(i) + 1) * r, last), 0))
    return [prev, cur, nxt]


def _scan(cx, h0, cw, cb, wa, ba, wx, bx, lam, layer, tile):
    b, s, w = cx.shape
    nt = s // tile
    fwd = lambda i: i
    rev = lambda i: nt - 1 - i
    return pl.pallas_call(
        _scan_kernel,
        grid=(b, nt),
        in_specs=(_halo_specs(tile, w, s, fwd) + _halo_specs(tile, w, s, rev)
                  + [pl.BlockSpec((None, 2, w), lambda bi, i: (bi, 0, 0)),
                     _slab((LRU_CONV_K, w), layer), _slab((1, w), layer),
                     _slab((2, w, w), layer), _slab((2, w), layer),
                     _slab((2, w, w), layer), _slab((2, w), layer), _slab((2, w), layer)]),
        out_specs=[pl.BlockSpec((None, tile, w), lambda bi, i: (bi, i, 0)),
                   pl.BlockSpec((None, tile, w), lambda bi, i: (bi, nt - 1 - i, 0))],
        out_shape=[jax.ShapeDtypeStruct((b, s, w), F32)] * 2,
        scratch_shapes=[pltpu.VMEM((2, w), F32),
                        pltpu.VMEM((2, w // LANES, tile, LANES), F32),
                        pltpu.VMEM((2, w // LANES, tile, LANES), F32),
                        pltpu.VMEM((2, w // LANES, tile // HALO, LANES), F32)],
        compiler_params=_params(2),
        name="lru_scan",
    )(cx, cx, cx, cx, cx, cx, h0, cw, cb, wa, ba, wx, bx, lam)


def _softmax_pv(parts, sink):
    m = None
    for s, _ in parts:
        mx = jnp.max(s, axis=-1, keepdims=True)
        m = mx if m is None else jnp.maximum(m, mx)
    if sink is not None:
        m = jnp.maximum(m, sink)
    den = None
    out = None
    for s, v in parts:
        p = jnp.exp2(s - m)
        sm = jnp.sum(p, axis=-1, keepdims=True)
        den = sm if den is None else den + sm
        pv = _dot(p.astype(BF16), v)
        out = pv if out is None else out + pv
    if sink is not None:
        den = den + jnp.exp2(sink - m)
    return out / den


def _dot_tn(a, b):
    return lax.dot_general(a, b, (((0,), (0,)), ((), ())), preferred_element_type=F32)


def _half_mask(shape, lower):
    lane = lax.broadcasted_iota(jnp.int32, shape, 1) % LANES
    return (lane < HEAD_DIM) if lower else (lane >= HEAD_DIM)


def _softmax_pv_t(parts, sink_row, lower):
    m = None
    for s, _ in parts:
        mx = jnp.max(s, axis=0, keepdims=True)
        m = mx if m is None else jnp.maximum(m, mx)
    if sink_row is not None:
        m = jnp.maximum(m, sink_row)
    res = None
    for s, v in parts:
        r = _dot_tn(v, jnp.exp2(s - m).astype(BF16))
        res = r if res is None else res + r
    out, den = (res[:HEAD_DIM], res[HEAD_DIM:HEAD_DIM + 1]) if lower else (res[HEAD_DIM:], res[0:1])
    if sink_row is not None:
        den = den + jnp.exp2(sink_row - m)
    return out * (1.0 / den)


WIN_Q_COLS = (0, 2, 1, 3)


def _window_kernel(qr_ref, qp_ref, kp_ref, kc_ref, kn_ref, vp_ref, vc_ref, vn_ref,
                   kx_ref, vx_ref, sink_ref, band_ref, o_ref):
    i = pl.program_id(1)
    nt = pl.num_programs(1)
    tq = qr_ref.shape[0]
    nblk = tq // WINDOW
    qr = qr_ref[...]
    qp = qp_ref[...]
    kcat = jnp.concatenate([kp_ref[...], kc_ref[...], kn_ref[...]], axis=0)
    vcat = jnp.concatenate([vp_ref[...], vc_ref[...], vn_ref[...]], axis=0)
    kx = kx_ref[...]
    vx = vx_ref[...]
    for hk in range(2):
        lower = hk == 0
        qm = _half_mask((tq, LANES), lower)
        zero = jnp.zeros((tq, LANES), BF16)
        qra, qrb = jnp.where(qm, qr[:, :LANES], zero), jnp.where(qm, qr[:, LANES:], zero)
        qpa, qpb = jnp.where(qm, qp[:, :LANES], zero), jnp.where(qm, qp[:, LANES:], zero)
        vw = jnp.where(_half_mask(vcat.shape, lower), vcat, jnp.ones_like(vcat))
        vxw = jnp.where(_half_mask(vx.shape, lower), vx, jnp.ones_like(vx))
        s_ctx = _dot_nt(kx, jnp.concatenate([qpa, qpb], axis=0))
        for j in range(nblk):
            rq = slice(j * WINDOW, (j + 1) * WINDOW)
            rk = slice(j * WINDOW, (j + 3) * WINDOW)
            if j == 0:
                band = band_ref[jnp.where(i == 0, 1, 0)]
            elif j == nblk - 1:
                band = band_ref[jnp.where(i == nt - 1, 2, 0)]
            else:
                band = band_ref[0]
            s_loc = (_dot_nt(kcat[rk], jnp.concatenate([qra[rq], qrb[rq]], axis=0))
                     + jnp.concatenate([band, band], axis=1))
            s_cx = jnp.concatenate([s_ctx[:, rq], s_ctx[:, tq + j * WINDOW:tq + (j + 1) * WINDOW]], axis=1)
            o = _softmax_pv_t([(s_loc, vw[rk]), (s_cx, vxw)], sink_ref[hk], lower).astype(o_ref.dtype)
            ha, hb = 2 * hk, 2 * hk + 1
            o_ref[ha * HEAD_DIM:(ha + 1) * HEAD_DIM, rq] = o[:, :WINDOW]
            o_ref[hb * HEAD_DIM:(hb + 1) * HEAD_DIM, rq] = o[:, WINDOW:]


def _window_band():
    kk = np.arange(3 * WINDOW)[:, None]
    r = np.arange(WINDOW)[None, :]
    ok = np.abs(kk - WINDOW - r) <= WINDOW
    variants = [ok, ok & (kk >= WINDOW), ok & (kk < 2 * WINDOW)]
    return jnp.asarray(np.where(np.stack(variants), 0.0, NEG_INF).astype(np.float32))


def _window_attention(qr, qp, k, v, kx, vx, sink_rows, layer, tile):
    b, s, qw = qr.shape
    kw = k.shape[2]
    lx = kx.shape[1]
    assert kw == LANES and qw == 2 * LANES and tile % WINDOW == 0 and s >= 2 * WINDOW
    r = tile // WINDOW
    last = s // WINDOW - 1
    band = _window_band()
    qspec = pl.BlockSpec((None, tile, qw), lambda bi, i: (bi, i, 0))
    prev = pl.BlockSpec((None, WINDOW, kw), lambda bi, i: (bi, jnp.maximum(i * r - 1, 0), 0))
    cur = pl.BlockSpec((None, tile, kw), lambda bi, i: (bi, i, 0))
    nxt = pl.BlockSpec((None, WINDOW, kw), lambda bi, i: (bi, jnp.minimum((i + 1) * r, last), 0))
    ctx = pl.BlockSpec((None, lx, kw), lambda bi, i: (bi, 0, 0))
    return pl.pallas_call(
        _window_kernel,
        grid=(b, s // tile),
        in_specs=[qspec, qspec, prev, cur, nxt, prev, cur, nxt, ctx, ctx,
                  _slab(sink_rows.shape[1:], layer), _full(band.shape)],
        out_specs=pl.BlockSpec((None, qw, tile), lambda bi, i: (bi, 0, i)),
        out_shape=jax.ShapeDtypeStruct((b, qw, s), BF16),
        compiler_params=_params(2),
        name="window_attention",
    )(qr, qp, k, k, k, v, v, v, kx, vx, sink_rows, band)


NBR_MASKED = 2 * NA_ROWS - 1


def _nbr_kernel(q_ref, kp_ref, kc_ref, kn_ref, vp_ref, vc_ref, vn_ref, kx_ref, vx_ref, sl_ref, sr_ref,
                o_ref, bias_ref, *, slab_index):
    i = pl.program_id(1)
    nt = pl.num_programs(1)
    tq = kp_ref.shape[0]
    n_sub = q_ref.shape[0] // tq
    tile_rows = tq // GRID_W

    @pl.when((pl.program_id(0) == 0) & (i == 0))
    def _():
        for v in range(3):
            for g in range(q_ref.shape[1] // LANES):
                for bj in range(3 * tile_rows):
                    for e in range(2):
                        for pp in range(tile_rows // 2):
                            left = slab_index[v][2 * pp][bj // tile_rows][bj % tile_rows]
                            right = slab_index[v][2 * pp + 1][bj // tile_rows][bj % tile_rows]
                            bias_ref[v, g, bj * GRID_W:(bj + 1) * GRID_W,
                                     e * tq + pp * LANES:e * tq + (pp + 1) * LANES] = (
                                sl_ref[2 * g + e, left] + sr_ref[2 * g + e, right])

    kcat = jnp.concatenate([kp_ref[...], kc_ref[...], kn_ref[...]], axis=0)
    vcat = jnp.concatenate([vp_ref[...], vc_ref[...], vn_ref[...]], axis=0)
    kx = kx_ref[...]
    vx = vx_ref[...]
    for sub in range(n_sub):
        variant = 1
        if sub == 0:
            variant = jnp.where(i == 0, 0, variant)
        if sub == n_sub - 1:
            variant = jnp.where(i == nt - 1, 2, variant)
        q = q_ref[sub * tq:(sub + 1) * tq, :]
        rk = slice(sub * tq, (sub + 3) * tq)
        for g in range(q.shape[1] // LANES):
            cols = slice(g * LANES, (g + 1) * LANES)
            qg, vg, vxg = q[:, cols], vcat[rk, cols], vx[:, cols]
            low = _half_mask(qg.shape, True)
            zero = jnp.zeros_like(qg)
            qcat = jnp.concatenate([jnp.where(low, qg, zero), jnp.where(low, zero, qg)], axis=0)
            s_loc = _dot_nt(kcat[rk, cols], qcat) + bias_ref[variant, g]
            s_ctx = _dot_nt(kx[:, cols], qcat)
            for e in range(2):
                lower = e == 0
                qs = slice(e * tq, (e + 1) * tq)
                vw = jnp.where(_half_mask(vg.shape, lower), vg, jnp.ones_like(vg))
                vxw = jnp.where(_half_mask(vxg.shape, lower), vxg, jnp.ones_like(vxg))
                o = _softmax_pv_t([(s_loc[:, qs], vw), (s_ctx[:, qs], vxw)], None, lower)
                h = 2 * g + e
                o_ref[h * HEAD_DIM:(h + 1) * HEAD_DIM, sub * tq:(sub + 1) * tq] = o.astype(o_ref.dtype)


def _nbr_slab_index(rows, tile_rows):
    n_tiles = rows // tile_rows
    index = []
    for ti in (0, 1, n_tiles - 1):
        r = ti * tile_rows + np.arange(tile_rows)
        start = np.clip(r - NA_ROWS // 2, 0, rows - NA_ROWS)
        blk = np.arange(3)[:, None] - 1 + ti
        krow = blk * tile_rows + np.arange(tile_rows)[None, :]
        ok = ((krow[None] >= start[:, None, None]) & (krow[None] < start[:, None, None] + NA_ROWS)
              & ((blk >= 0) & (blk < n_tiles))[None])
        dr = krow[None] - r[:, None, None] + NA_ROWS - 1
        assert ((dr >= 0) & (dr < NBR_MASKED))[ok].all()
        index.append(np.where(ok, dr, NBR_MASKED))
    return np.stack(index).tolist()


def _nbr_slabs(rel_bias):
    qc = np.arange(GRID_W)[None, :]
    kc = np.arange(GRID_W)[:, None]
    dc = np.clip(kc - qc, 1 - NA_COLS, NA_COLS - 1) + NA_COLS - 1
    c_start = np.clip(qc - NA_COLS // 2, 0, GRID_W - NA_COLS)
    col_ok = (kc >= c_start) & (kc < c_start + NA_COLS)
    one_hot = jnp.asarray(np.eye(2 * NA_COLS - 1, dtype=np.float32)[dc])
    slabs = jnp.einsum('kqc,lhrc->lhrkq', one_hot, rel_bias.astype(F32), precision=lax.Precision.HIGHEST)
    slabs = jnp.where(jnp.asarray(col_ok), slabs, NEG_INF)
    slabs = jnp.concatenate([slabs, jnp.full_like(slabs[:, :, :1], NEG_INF)], axis=2)
    zeros = jnp.zeros_like(slabs)
    return jnp.concatenate([slabs, zeros], axis=-1), jnp.concatenate([zeros, slabs], axis=-1)


def _nbr_attention(q, k, v, kx, vx, slabs, slab_index, layer, tile, n_sub):
    b, s, w = q.shape
    lx = kx.shape[1]
    step = n_sub * tile
    nt = s // step
    last = s // tile - 1
    left, right = slabs
    tok = pl.BlockSpec((None, step, w), lambda bi, i: (bi, i, 0))
    prev = pl.BlockSpec((None, tile, w), lambda bi, i: (bi, jnp.maximum(i * n_sub - 1, 0), 0))
    nxt = pl.BlockSpec((None, tile, w), lambda bi, i: (bi, jnp.minimum((i + 1) * n_sub, last), 0))
    ctx = pl.BlockSpec((None, lx, w), lambda bi, i: (bi, 0, 0))
    return pl.pallas_call(
        functools.partial(_nbr_kernel, slab_index=slab_index),
        grid=(b, nt),
        in_specs=[tok, prev, tok, nxt, prev, tok, nxt, ctx, ctx,
                  _slab(left.shape[1:], layer), _slab(right.shape[1:], layer)],
        out_specs=pl.BlockSpec((None, w, step), lambda bi, i: (bi, 0, i)),
        out_shape=jax.ShapeDtypeStruct((b, w, s), BF16),
        scratch_shapes=[pltpu.VMEM((3, w // LANES, 3 * tile, 2 * tile), F32)],
        compiler_params=_params(2),
        name="nbr_attention",
    )(q, k, k, k, v, v, v, kx, vx, left, right)


def _ctx_attn_kernel(q_ref, k_ref, v_ref, sink_ref, o_ref, *, n_kv, group, use_sink, q_cols):
    q = q_ref[...]
    k = k_ref[...]
    v = v_ref[...]
    l = q.shape[0]
    outs = []
    for hk in range(n_kv):
        ks = slice(hk * HEAD_DIM, (hk + 1) * HEAD_DIM)
        for h in range(hk * group, (hk + 1) * group):
            s = _dot_nt(q[:, q_cols[h] * HEAD_DIM:(q_cols[h] + 1) * HEAD_DIM], k[:, ks])
            sink = jnp.broadcast_to(sink_ref[0:1, h:h + 1], (l, 1)) if use_sink else None
            outs.append(_softmax_pv([(s, v[:, ks])], sink))
    o_ref[...] = jnp.concatenate(outs, axis=1).astype(o_ref.dtype)


def _ctx_attention(q, k, v, sink, layer, use_sink, q_cols):
    b, l, qw = q.shape
    kw = k.shape[2]
    n_kv = kw // HEAD_DIM
    group = (qw // HEAD_DIM) // n_kv
    return pl.pallas_call(
        functools.partial(_ctx_attn_kernel, n_kv=n_kv, group=group, use_sink=use_sink, q_cols=q_cols),
        grid=(b,),
        in_specs=[pl.BlockSpec((None, l, qw), lambda bi: (bi, 0, 0)),
                  pl.BlockSpec((None, l, kw), lambda bi: (bi, 0, 0)),
                  pl.BlockSpec((None, l, kw), lambda bi: (bi, 0, 0)),
                  _slab(sink.shape[1:], layer)],
        out_specs=pl.BlockSpec((None, l, qw), lambda bi: (bi, 0, 0)),
        out_shape=jax.ShapeDtypeStruct((b, l, qw), BF16),
        compiler_params=_params(1),
        name="ctx_attention",
    )(q, k, v, sink)


def _merge_kernel(x_ref, mod_ref, g_ref, pp_ref, pc_ref, pn_ref, ab_ref, caw_ref,
                  yb_ref, hf_ref, hb_ref, cg_ref, yd_ref,
                  bg_ref, wbr_ref, wo_ref, *rest, sub):
    wgl_refs, o_ref = rest[:-1], rest[-1]
    i = pl.program_id(1)
    nt = pl.num_programs(1)
    t, d = x_ref.shape
    prev = jnp.where(i > 0, pp_ref[...], 0.0)
    nxt = jnp.where(i < nt - 1, pn_ref[...], 0.0)
    ext = jnp.concatenate([prev, pc_ref[...], nxt], axis=0)
    for r0 in range(0, t, sub):
        rows = slice(r0, r0 + sub)
        x = x_ref[rows, :]
        h = _adaln(x, g_ref[...], mod_ref[3:4, :], mod_ref[4:5, :]).astype(BF16)
        conv = (ext[HALO - 1 + r0:HALO - 1 + r0 + sub] * caw_ref[0:1, :]
                + ext[HALO + r0:HALO + r0 + sub] * caw_ref[1:2, :]
                + ext[HALO + 1 + r0:HALO + 1 + r0 + sub] * caw_ref[2:3, :])
        y_a = (ab_ref[rows, :] * conv).astype(BF16)
        y_c = ((hf_ref[rows, :] + hb_ref[rows, :]) * _gelu_tanh(cg_ref[rows, :])).astype(BF16)
        ys = ((y_a, _dot), (yb_ref[:, rows], _dot_tn), (y_c, _dot), (yd_ref[:, rows], _dot_tn))
        out = None
        for c0 in range(0, d, MERGE_CHUNK):
            merged = None
            for n, (y, dot) in enumerate(ys):
                gc = slice(n * d + c0, n * d + c0 + MERGE_CHUNK)
                gate = _sigmoid(_dot(h, wgl_refs[(n * d + c0) // MERGE_CHUNK][...]) + bg_ref[0:1, gc])
                term = gate * dot(y, wbr_ref[n, :, c0:c0 + MERGE_CHUNK])
                merged = term if merged is None else merged + term
            part = _dot(merged.astype(BF16), wo_ref[c0:c0 + MERGE_CHUNK, :])
            out = part if out is None else out + part
        o_ref[rows, :] = x + mod_ref[5:6, :] * out


def _merge(x, mods, norm_g, pa, ab, caw, yb, hf, hb, cg, yd, w_in, bg, wbr, wo, layer, ctx_row, tile, sub):
    b, s, d = x.shape
    mw = pa.shape[2]
    n_gate = N_BRANCH * d
    first, rem = divmod(w_in.shape[2] - n_gate, MERGE_CHUNK)
    assert rem == 0 and d % MERGE_CHUNK == 0
    gate_specs = [pl.BlockSpec((None, d, MERGE_CHUNK), lambda *_, k=k: (layer, 0, first + k),
                               pipeline_mode=pl.Buffered(1)) for k in range(n_gate // MERGE_CHUNK)]
    tok = lambda width: pl.BlockSpec((None, tile, width), lambda bi, i: (bi, i, 0))
    tok_t = pl.BlockSpec((None, mw, tile), lambda bi, i: (bi, 0, i))
    return pl.pallas_call(
        functools.partial(_merge_kernel, sub=min(sub, tile)),
        grid=(b, s // tile),
        in_specs=([tok(d), _mod_spec(d, layer, ctx_row), _slab((1, d), layer, 1)]
                  + _halo_specs(tile, mw, s, lambda i: i)
                  + [tok(mw), _slab((CONV_K, mw), layer),
                     tok_t, tok(mw), tok(mw), tok(mw), tok_t,
                     _slab(bg.shape[1:], layer), _slab(wbr.shape[1:], layer), _slab(wo.shape[1:], layer)]
                  + gate_specs),
        out_specs=tok(d),
        out_shape=jax.ShapeDtypeStruct((b, s, d), F32),
        compiler_params=_params(2),
        name="merge",
    )(x, mods, norm_g, pa, pa, pa, ab, caw, yb, hf, hb, cg, yd, bg, wbr, wo, *([w_in] * len(gate_specs)))


def _rope_tables(n_tok, width):
    half = HEAD_DIM // 2
    nf = half // 2
    inv_freq = ROPE_BASE ** (-jnp.arange(nf, dtype=F32) / nf)
    pos = jnp.arange(n_tok)
    ang_r = (pos // GRID_W).astype(F32)[:, None] * inv_freq[None, :]
    ang_c = (pos % GRID_W).astype(F32)[:, None] * inv_freq[None, :]
    cos = jnp.concatenate([jnp.cos(ang_r)] * 2 + [jnp.cos(ang_c)] * 2, axis=1)
    sin = jnp.concatenate([-jnp.sin(ang_r), jnp.sin(ang_r), -jnp.sin(ang_c), jnp.sin(ang_c)], axis=1)
    reps = width // HEAD_DIM
    return jnp.tile(cos, (1, reps)), jnp.tile(sin, (1, reps))


def _block_diag(w):
    *lead, n, c, e = w.shape
    eye = jnp.eye(n, dtype=w.dtype)
    return (eye[:, None, :, None] * w[..., :, :, None, :]).reshape(*lead, n * c, n * e)


def _pick_tile(n, target):
    t = min(n, target)
    while n % t:
        t //= 2
    return t


def kernel(x, c, ctx, c_ctx, w_mod, b_mod, norm_g, ffn_w_gate, ffn_w_up, ffn_w_down, w_in, b_gate,
           conv_a_w, qk_norm_g, attn_sink, lru_conv_w, lru_conv_b, lru_w_a, lru_b_a, lru_w_x, lru_b_x,
           lru_lam, na_rel_bias, w_branch, w_out):
    b, s, d = x.shape
    lx = ctx.shape[1]
    depth = w_mod.shape[0]
    mw = d // 4
    kvw = mw // 2
    rows = s // GRID_W
    assert s % GRID_W == 0 and rows >= 4 * NA_ROWS and lx % HALO == 0

    tile = _pick_tile(s, 512)
    tile_big = _pick_tile(s, 1024)
    tile_x = _pick_tile(lx, 512)
    tile_win = _pick_tile(s, 512)
    nbr_rows = 4
    tile_nbr = nbr_rows * GRID_W

    cs = jnp.zeros((8, d), F32).at[:b].set(c).at[b].set(c_ctx)
    mods = _modulation(cs, w_mod, b_mod).reshape(depth, 8, N_MOD, d)
    lat, cx_row = None, b

    cos, sin = _rope_tables(s, kvw)
    cos_x, sin_x = jnp.ones((lx, kvw), F32), jnp.zeros((lx, kvw), F32)
    lane = np.arange(mw)
    mhead = jnp.asarray((lane[:, None] // HEAD_DIM == lane[None, :] // HEAD_DIM) / HEAD_DIM, dtype=BF16)

    ng = norm_g.reshape(depth, 3, 1, d)
    wg, wu, wd = ffn_w_gate.astype(BF16), ffn_w_up.astype(BF16), ffn_w_down.astype(BF16)
    w_in_b = w_in.astype(BF16)
    qkg = jnp.tile(qk_norm_g, (1, 1, mw // HEAD_DIM))
    sink = (attn_sink * LOG2E).reshape(depth, 1, -1)
    sink_rows = jnp.repeat(sink.reshape(depth, 2, 1, 2), WINDOW, axis=3)
    lru_par = (lru_conv_w, lru_conv_b.reshape(depth, 1, mw), _block_diag(lru_w_a).astype(BF16), lru_b_a,
               _block_diag(lru_w_x).astype(BF16), lru_b_x, lru_lam)
    slabs = _nbr_slabs(na_rel_bias * LOG2E)
    slab_index = _nbr_slab_index(rows, nbr_rows)
    bg = b_gate.reshape(depth, 1, -1)
    wbr = w_branch.astype(BF16)
    wo = w_out.astype(BF16)

    xc = ctx
    for l in range(depth):
        ctx_out = l < depth - 1
        x = _ffn(x, mods, ng, wg, wu, wd, l, 0, lat, tile_big, tile)
        xc = _ffn(xc, mods, ng, wg, wu, wd, l, 0, cx_row, tile_x, tile_x)
        (pa, ab, qr, qp, kr, vb, cxl, cg, dq, dk, dv) = _inproj(
            x, mods, ng, w_in_b, qkg, mhead, cos, sin, l, lat, tile_big, tile)
        (pa_c, ab_c, _, qp_c, k_c, vb_c, cx_c, cg_c, dq_c, dk_c, dv_c) = _inproj(
            xc, mods, ng, w_in_b, qkg, mhead, cos_x, sin_x, l, cx_row, tile_x, tile_x)

        hf_c, hb_c = _scan(cx_c, jnp.zeros((b, 2, mw), F32), *lru_par, l, tile_x)
        h0 = jnp.stack([hf_c[:, lx - 1], hb_c[:, 0]], axis=1)
        hf, hb = _scan(cxl, h0, *lru_par, l, tile)
        yb = _window_attention(qr, qp, kr, vb, k_c, vb_c, sink_rows, l, tile_win)
        yd = _nbr_attention(dq, dk, dv, dk_c, dv_c, slabs, slab_index, l, tile_nbr, 2)
        x = _merge(x, mods, ng, pa, ab, conv_a_w, yb, hf, hb, cg, yd, w_in_b, bg, wbr, wo, l, lat, tile_big, tile)
        x = _ffn(x, mods, ng, wg, wu, wd, l, 1, lat, tile_big, tile)
        if ctx_out:
            yb_c = jnp.swapaxes(_ctx_attention(qp_c, k_c, vb_c, sink, l, True, WIN_Q_COLS), 1, 2)
            yd_c = jnp.swapaxes(_ctx_attention(dq_c, dk_c, dv_c, sink, l, False, (0, 1, 2, 3)), 1, 2)
            xc = _merge(xc, mods, ng, pa_c, ab_c, conv_a_w, yb_c, hf_c, hb_c, cg_c, yd_c,
                        w_in_b, bg, wbr, wo, l, cx_row, tile_x, tile_x)
            xc = _ffn(xc, mods, ng, wg, wu, wd, l, 1, cx_row, tile_x, tile_x)
    return x
```

```python
import functools

import jax
import jax.numpy as jnp
import numpy as np
from jax import lax
from jax.experimental import pallas as pl
from jax.experimental.pallas import tpu as pltpu

HEAD_DIM = 64
GRID_W = 64
WINDOW = 128
NA_ROWS = 8
NA_COLS = 16
N_BRANCH = 4
CONV_K = 3
LRU_CONV_K = 4
LRU_C = 8.0
ROPE_BASE = 10000.0
EPS = 1e-6
NEG_INF = -1e30
N_MOD = 9
LOG2E = 1.4426950408889634

V7X_VMEM_LIMIT_BYTES = 60 * 1024 * 1024
HALO = 8
LANES = 128
MERGE_CHUNK = 512

F32 = jnp.float32
BF16 = jnp.bfloat16


def _params(n_grid):
    return pltpu.CompilerParams(
        dimension_semantics=("arbitrary",) * n_grid,
        vmem_limit_bytes=V7X_VMEM_LIMIT_BYTES)


def _dot(a, b):
    return jnp.dot(a, b, preferred_element_type=F32)


def _dot_nt(a, b):
    return lax.dot_general(a, b, (((1,), (1,)), ((), ())), preferred_element_type=F32)


def _sigmoid(x):
    return jax.nn.sigmoid(x)


def _gelu_tanh(x):
    return 0.5 * x * (1.0 + jnp.tanh(0.7978845608028654 * (x + 0.044715 * (x * x * x))))


def _adaln(x, g, shift, scale):
    ms = jnp.mean(x * x, axis=-1, keepdims=True)
    return (x * lax.rsqrt(ms + EPS) * g) * (1.0 + scale) + shift


def _full(shape):
    n = len(shape)
    return pl.BlockSpec(shape, lambda *_: (0,) * n)


def _slab(shape, *lead):
    n = len(shape)
    return pl.BlockSpec((None,) * len(lead) + tuple(shape), lambda *_: tuple(lead) + (0,) * n,
                        pipeline_mode=pl.Buffered(1))


def _mod_spec(d, layer, ctx_row):
    if ctx_row is None:
        return pl.BlockSpec((None, None, N_MOD, d), lambda bi, i: (layer, bi, 0, 0))
    return pl.BlockSpec((None, None, N_MOD, d), lambda bi, i: (layer, ctx_row, 0, 0))


def _mod_kernel(ct_ref, w_ref, b_ref, o_ref, *, n_rows):
    ct = ct_ref[...]
    st = ct * _sigmoid(ct)
    w = w_ref[...]
    rows = [jnp.sum(w * st[:, r:r + 1], axis=0, keepdims=True) for r in range(n_rows)]
    rows += [jnp.zeros_like(rows[0])] * (o_ref.shape[0] - n_rows)
    o_ref[...] = jnp.concatenate(rows, axis=0) + b_ref[...]


def _modulation(cs_t, n_rows, w_mod, b_mod):
    depth, d, n = w_mod.shape
    tn = n // 8
    return pl.pallas_call(
        functools.partial(_mod_kernel, n_rows=n_rows),
        grid=(depth, n // tn),
        in_specs=[pl.BlockSpec((d, 8), lambda l, j: (0, 0)),
                  pl.BlockSpec((None, d, tn), lambda l, j: (l, 0, j)),
                  pl.BlockSpec((None, 1, tn), lambda l, j: (l, 0, j))],
        out_specs=pl.BlockSpec((None, 8, tn), lambda l, j: (l, 0, j)),
        out_shape=jax.ShapeDtypeStruct((depth, 8, n), F32),
        compiler_params=_params(2),
        name="modulation",
    )(cs_t, w_mod, b_mod.reshape(depth, 1, n))


def _ffn_kernel(x_ref, mod_ref, g_ref, wg_ref, wu_ref, wd_ref, o_ref, *, row0, chunks, sub):
    for r0 in range(0, x_ref.shape[0], sub):
        rows = slice(r0, r0 + sub)
        x = x_ref[rows, :]
        h = _adaln(x, g_ref[...], mod_ref[row0:row0 + 1, :], mod_ref[row0 + 1:row0 + 2, :]).astype(BF16)
        acc = None
        for (lo, hi) in chunks:
            a = _dot(h, wg_ref[:, lo:hi])
            u = _dot(h, wu_ref[:, lo:hi])
            act = (a * _sigmoid(a) * u).astype(BF16)
            part = _dot(act, wd_ref[lo:hi, :])
            acc = part if acc is None else acc + part
        o_ref[rows, :] = x + (0.5 * mod_ref[row0 + 2:row0 + 3, :]) * acc


def _ffn(x, mods, norm_g, wg, wu, wd, layer, which, ctx_row, tile, sub):
    b, s, d = x.shape
    f = wg.shape[3]
    half = (f // 2 + 255) // 256 * 256
    chunks = ((0, half), (half, f)) if half < f else ((0, f),)
    return pl.pallas_call(
        functools.partial(_ffn_kernel, row0=6 * which, chunks=chunks, sub=min(sub, tile)),
        grid=(b, s // tile),
        in_specs=[pl.BlockSpec((None, tile, d), lambda bi, i: (bi, i, 0)),
                  _mod_spec(d, layer, ctx_row),
                  _slab((1, d), layer, 2 * which),
                  _slab((d, f), layer, which), _slab((d, f), layer, which), _slab((f, d), layer, which)],
        out_specs=pl.BlockSpec((None, tile, d), lambda bi, i: (bi, i, 0)),
        out_shape=jax.ShapeDtypeStruct((b, s, d), F32),
        compiler_params=_params(2),
        name="ffn",
    )(x, mods, norm_g, wg, wu, wd)


def _head_rms(t, m_ref, g):
    n = t.shape[1]
    sq = t * t
    hi = sq.astype(BF16)
    lo = (sq - hi.astype(F32)).astype(BF16)
    m = m_ref[0:n, 0:n]
    ms = _dot(hi, m) + _dot(lo, m)
    return t * lax.rsqrt(ms + EPS) * g


def _rope(t, cos, sin_signed):
    n = t.shape[1]
    lane = lax.broadcasted_iota(jnp.int32, t.shape, 1)
    partner = jnp.where((lane % 32) < 16, pltpu.roll(t, n - 16, 1), pltpu.roll(t, 16, 1))
    return t * cos + partner * sin_signed


def _inproj_kernel(x_ref, mod_ref, g_ref, w_ref, qkg_ref, m_ref, cos_ref, sin_ref,
                   pa_ref, ab_ref, qr_ref, qp_ref, kr_ref, vb_ref, cx_ref, cg_ref,
                   dq_ref, dk_ref, dv_ref, *, mw, sub):
    kvw = mw // 2
    scale = LOG2E * HEAD_DIM ** -0.5
    for r0 in range(0, x_ref.shape[0], sub):
        rows = slice(r0, r0 + sub)
        h = _adaln(x_ref[rows, :], g_ref[...], mod_ref[3:4, :], mod_ref[4:5, :]).astype(BF16)
        u = _dot(h, w_ref[...])
        o = 0
        ax = u[:, o:o + mw]; o += mw
        ab = u[:, o:o + mw]; o += mw
        ac = u[:, o:o + mw]; o += mw
        bq = u[:, o:o + mw]; o += mw
        qa, qb = bq[:, :LANES], bq[:, LANES:]
        low = _half_mask(qa.shape, True)
        bq = jnp.concatenate([jnp.where(low, qa, pltpu.roll(qb, HEAD_DIM, 1)),
                              jnp.where(low, pltpu.roll(qa, HEAD_DIM, 1), qb)], axis=1)
        bk = u[:, o:o + kvw]; o += kvw
        bv = u[:, o:o + kvw]; o += kvw
        cx = u[:, o:o + mw]; o += mw
        cg = u[:, o:o + mw]; o += mw
        dq = u[:, o:o + mw]; o += mw
        dk = u[:, o:o + mw]; o += mw
        dv = u[:, o:o + mw]; o += mw
        pa_ref[rows, :] = ac * ax
        ab_ref[rows, :] = ab
        cx_ref[rows, :] = cx
        cg_ref[rows, :] = cg
        cos = cos_ref[rows, :]
        sin = sin_ref[rows, :]
        cos2 = jnp.concatenate([cos, cos], axis=1)
        sin2 = jnp.concatenate([sin, sin], axis=1)
        q = _head_rms(bq, m_ref, qkg_ref[0:1, :]) * scale
        qp_ref[rows, :] = q.astype(BF16)
        qr_ref[rows, :] = _rope(q, cos2, sin2).astype(BF16)
        k = _head_rms(bk, m_ref, qkg_ref[1:2, 0:kvw])
        kr_ref[rows, :] = _rope(k, cos, sin).astype(BF16)
        vb_ref[rows, :] = bv.astype(BF16)
        dq_ref[rows, :] = (_head_rms(dq, m_ref, qkg_ref[2:3, :]) * scale).astype(BF16)
        dk_ref[rows, :] = _head_rms(dk, m_ref, qkg_ref[3:4, :]).astype(BF16)
        dv_ref[rows, :] = dv.astype(BF16)


def _inproj(x, mods, norm_g, w_in, qkg, mhead, cos, sin, layer, ctx_row, tile, sub):
    b, s, d = x.shape
    mw = qkg.shape[2]
    kvw = mw // 2
    nw = 9 * mw + 2 * kvw
    assert mw == 2 * LANES
    tok = lambda width: pl.BlockSpec((None, tile, width), lambda bi, i: (bi, i, 0))
    widths = (mw, mw, mw, mw, kvw, kvw, mw, mw, mw, mw, mw)
    dtypes = (F32, F32, BF16, BF16, BF16, BF16, F32, F32, BF16, BF16, BF16)
    return pl.pallas_call(
        functools.partial(_inproj_kernel, mw=mw, sub=min(sub, tile)),
        grid=(b, s // tile),
        in_specs=[tok(d),
                  _mod_spec(d, layer, ctx_row),
                  _slab((1, d), layer, 1), _slab((d, nw), layer), _slab((4, mw), layer), _full((mw, mw)),
                  pl.BlockSpec((tile, kvw), lambda bi, i: (i, 0)),
                  pl.BlockSpec((tile, kvw), lambda bi, i: (i, 0))],
        out_specs=[tok(wd) for wd in widths],
        out_shape=[jax.ShapeDtypeStruct((b, s, wd), dt) for wd, dt in zip(widths, dtypes)],
        compiler_params=_params(2),
        name="inproj",
    )(x, mods, norm_g, w_in, qkg, mhead, cos, sin)


def _shift_rows(t, d, fill, reverse):
    n = t.shape[0]
    row = lax.broadcasted_iota(jnp.int32, t.shape, 0)
    if reverse:
        return jnp.where(row < n - d, pltpu.roll(t, n - d, 0), fill)
    return jnp.where(row >= d, pltpu.roll(t, d, 0), fill)


def _lru_coeffs(prev_ref, cur_ref, next_ref, has_prev, has_next, cw_ref, cb_ref,
                wa_ref, ba_ref, wx_ref, bx_ref, lam_ref, direction):
    cur = cur_ref[...]
    t = cur.shape[0]
    prev = jnp.where(has_prev, prev_ref[...], 0.0)
    nxt = jnp.where(has_next, next_ref[...], 0.0)
    ext = jnp.concatenate([prev, cur, nxt], axis=0)
    u = cb_ref[...] + cur * cw_ref[2:3, :]
    u = u + ext[HALO - 2:HALO - 2 + t] * cw_ref[0:1, :]
    u = u + ext[HALO - 1:HALO - 1 + t] * cw_ref[1:2, :]
    u = u + ext[HALO + 1:HALO + 1 + t] * cw_ref[3:4, :]
    ub = u.astype(BF16)
    r = _sigmoid(_dot(ub, wa_ref[direction]) + ba_ref[direction:direction + 1, :])
    ig = _sigmoid(_dot(ub, wx_ref[direction]) + bx_ref[direction:direction + 1, :])
    nl = -lam_ref[direction:direction + 1, :]
    softplus = jnp.maximum(nl, 0.0) + jnp.log1p(jnp.exp(-jnp.abs(nl)))
    log_a = (-LRU_C) * r * softplus
    a = jnp.exp(log_a)
    return a, jnp.sqrt(-jnp.tanh(log_a) * (a * a + 1.0)) * (ig * u)


def _scan_tile(a, bb, carry, reverse, sa_ref, sb_ref, sc_ref, out_ref):
    t, w = a.shape
    g = t // HALO
    a3 = a.reshape(g, HALO, w)
    b3 = bb.reshape(g, HALO, w)
    sub = lax.broadcasted_iota(jnp.int32, (g, HALO, w), 1)
    d = 1
    while d < HALO:
        ok = (sub < HALO - d) if reverse else (sub >= d)
        shift = HALO - d if reverse else d
        a_r = pltpu.roll(a3, shift, 1)
        b_r = pltpu.roll(b3, shift, 1)
        b3 = jnp.where(ok, a3 * b_r + b3, b3)
        a3 = jnp.where(ok, a3 * a_r, a3)
        d *= 2
    sa_ref[...] = a3.reshape(t, w)
    sb_ref[...] = b3.reshape(t, w)
    edge = 0 if reverse else HALO - 1
    at = sa_ref[pl.ds(edge, g, stride=HALO), :]
    bt = sb_ref[pl.ds(edge, g, stride=HALO), :]
    d = 1
    while d < g:
        a_s = _shift_rows(at, d, 1.0, reverse)
        b_s = _shift_rows(bt, d, 0.0, reverse)
        bt = at * b_s + bt
        at = at * a_s
        d *= 2
    h_end = at * carry + bt
    sc_ref[...] = _shift_rows(h_end, 1, carry, reverse)
    for gi in range(g):
        rows = slice(gi * HALO, (gi + 1) * HALO)
        out_ref[rows, :] = sa_ref[rows, :] * sc_ref[gi:gi + 1, :] + sb_ref[rows, :]
    return h_end[0:1, :] if reverse else h_end[g - 1:g, :]


def _scan_kernel(fp_ref, fc_ref, fn_ref, rp_ref, rc_ref, rn_ref, h0_ref, cw_ref, cb_ref,
                 wa_ref, ba_ref, wx_ref, bx_ref, lam_ref, hf_ref, hb_ref,
                 carry_ref, sa_ref, sb_ref, sc_ref):
    i = pl.program_id(1)
    nt = pl.num_programs(1)

    @pl.when(i == 0)
    def _():
        carry_ref[...] = h0_ref[...]

    par = (cw_ref, cb_ref, wa_ref, ba_ref, wx_ref, bx_ref, lam_ref)
    tiles = ((fp_ref, fc_ref, fn_ref, i > 0, i < nt - 1, hf_ref),
             (rp_ref, rc_ref, rn_ref, i < nt - 1, i > 0, hb_ref))
    for direction, (p_ref, c_ref, n_ref, has_prev, has_next, out_ref) in enumerate(tiles):
        a, bb = _lru_coeffs(p_ref, c_ref, n_ref, has_prev, has_next, *par, direction)
        for j in range(a.shape[1] // LANES):
            cols = slice(j * LANES, (j + 1) * LANES)
            carry_ref[direction:direction + 1, cols] = _scan_tile(
                a[:, cols], bb[:, cols], carry_ref[direction:direction + 1, cols], direction == 1,
                sa_ref.at[direction, j], sb_ref.at[direction, j], sc_ref.at[direction, j],
                out_ref.at[:, cols])


def _halo_specs(tile, width, n_rows, tile_index):
    r = tile // HALO
    last = n_rows // HALO - 1
    prev = pl.BlockSpec((None, HALO, width),
                        lambda bi, i: (bi, jnp.maximum(tile_index(i) * r - 1, 0), 0))
    cur = pl.BlockSpec((None, tile, width), lambda bi, i: (bi, tile_index(i), 0))
    nxt = pl.BlockSpec((None, HALO, width),
                       lambda bi, i: (bi, jnp.minimum((tile_index(i) + 1) * r, last), 0))
    return [prev, cur, nxt]


def _scan(cx, h0, cw, cb, wa, ba, wx, bx, lam, layer, tile):
    b, s, w = cx.shape
    nt = s // tile
    fwd = lambda i: i
    rev = lambda i: nt - 1 - i
    return pl.pallas_call(
        _scan_kernel,
        grid=(b, nt),
        in_specs=(_halo_specs(tile, w, s, fwd) + _halo_specs(tile, w, s, rev)
                  + [pl.BlockSpec((None, 2, w), lambda bi, i: (bi, 0, 0)),
                     _slab((LRU_CONV_K, w), layer), _slab((1, w), layer),
                     _slab((2, w, w), layer), _slab((2, w), layer),
                     _slab((2, w, w), layer), _slab((2, w), layer), _slab((2, w), layer)]),
        out_specs=[pl.BlockSpec((None, tile, w), lambda bi, i: (bi, i, 0)),
                   pl.BlockSpec((None, tile, w), lambda bi, i: (bi, nt - 1 - i, 0))],
        out_shape=[jax.ShapeDtypeStruct((b, s, w), F32)] * 2,
        scratch_shapes=[pltpu.VMEM((2, w), F32),
                        pltpu.VMEM((2, w // LANES, tile, LANES), F32),
                        pltpu.VMEM((2, w // LANES, tile, LANES), F32),
                        pltpu.VMEM((2, w // LANES, tile // HALO, LANES), F32)],
        compiler_params=_params(2),
        name="lru_scan",
    )(cx, cx, cx, cx, cx, cx, h0, cw, cb, wa, ba, wx, bx, lam)


def _softmax_pv(parts, sink):
    m = None
    for s, _ in parts:
        mx = jnp.max(s, axis=-1, keepdims=True)
        m = mx if m is None else jnp.maximum(m, mx)
    if sink is not None:
        m = jnp.maximum(m, sink)
    den = None
    out = None
    for s, v in parts:
        p = jnp.exp2(s - m)
        sm = jnp.sum(p, axis=-1, keepdims=True)
        den = sm if den is None else den + sm
        pv = _dot(p.astype(BF16), v)
        out = pv if out is None else out + pv
    if sink is not None:
        den = den + jnp.exp2(sink - m)
    return out / den


def _dot_tn(a, b):
    return lax.dot_general(a, b, (((0,), (0,)), ((), ())), preferred_element_type=F32)


def _half_mask(shape, lower):
    lane = lax.broadcasted_iota(jnp.int32, shape, 1) % LANES
    return (lane < HEAD_DIM) if lower else (lane >= HEAD_DIM)


def _softmax_pv_t(parts, sink_row, lower):
    m = None
    for s, _ in parts:
        mx = jnp.max(s, axis=0, keepdims=True)
        m = mx if m is None else jnp.maximum(m, mx)
    if sink_row is not None:
        m = jnp.maximum(m, sink_row)
    res = None
    for s, v in parts:
        r = _dot_tn(v, jnp.exp2(s - m).astype(BF16))
        res = r if res is None else res + r
    out, den = (res[:HEAD_DIM], res[HEAD_DIM:HEAD_DIM + 1]) if lower else (res[HEAD_DIM:], res[0:1])
    if sink_row is not None:
        den = den + jnp.exp2(sink_row - m)
    return out * (1.0 / den)


WIN_Q_COLS = (0, 2, 1, 3)


def _window_kernel(qr_ref, qp_ref, kp_ref, kc_ref, kn_ref, vp_ref, vc_ref, vn_ref,
                   kx_ref, vx_ref, sink_ref, band_ref, o_ref):
    i = pl.program_id(1)
    nt = pl.num_programs(1)
    tq = qr_ref.shape[0]
    nblk = tq // WINDOW
    qr = qr_ref[...]
    qp = qp_ref[...]
    kcat = jnp.concatenate([kp_ref[...], kc_ref[...], kn_ref[...]], axis=0)
    vcat = jnp.concatenate([vp_ref[...], vc_ref[...], vn_ref[...]], axis=0)
    kx = kx_ref[...]
    vx = vx_ref[...]
    for hk in range(2):
        lower = hk == 0
        qm = _half_mask((tq, LANES), lower)
        zero = jnp.zeros((tq, LANES), BF16)
        qra, qrb = jnp.where(qm, qr[:, :LANES], zero), jnp.where(qm, qr[:, LANES:], zero)
        qpa, qpb = jnp.where(qm, qp[:, :LANES], zero), jnp.where(qm, qp[:, LANES:], zero)
        vw = jnp.where(_half_mask(vcat.shape, lower), vcat, jnp.ones_like(vcat))
        vxw = jnp.where(_half_mask(vx.shape, lower), vx, jnp.ones_like(vx))
        s_ctx = _dot_nt(kx, jnp.concatenate([qpa, qpb], axis=0))
        for j in range(nblk):
            rq = slice(j * WINDOW, (j + 1) * WINDOW)
            rk = slice(j * WINDOW, (j + 3) * WINDOW)
            if j == 0:
                band = band_ref[jnp.where(i == 0, 1, 0)]
            elif j == nblk - 1:
                band = band_ref[jnp.where(i == nt - 1, 2, 0)]
            else:
                band = band_ref[0]
            s_loc = (_dot_nt(kcat[rk], jnp.concatenate([qra[rq], qrb[rq]], axis=0))
                     + jnp.concatenate([band, band], axis=1))
            s_cx = jnp.concatenate([s_ctx[:, rq], s_ctx[:, tq + j * WINDOW:tq + (j + 1) * WINDOW]], axis=1)
            o = _softmax_pv_t([(s_loc, vw[rk]), (s_cx, vxw)], sink_ref[hk], lower).astype(o_ref.dtype)
            ha, hb = 2 * hk, 2 * hk + 1
            o_ref[ha * HEAD_DIM:(ha + 1) * HEAD_DIM, rq] = o[:, :WINDOW]
            o_ref[hb * HEAD_DIM:(hb + 1) * HEAD_DIM, rq] = o[:, WINDOW:]


def _window_band():
    kk = np.arange(3 * WINDOW)[:, None]
    r = np.arange(WINDOW)[None, :]
    ok = np.abs(kk - WINDOW - r) <= WINDOW
    variants = [ok, ok & (kk >= WINDOW), ok & (kk < 2 * WINDOW)]
    return jnp.asarray(np.where(np.stack(variants), 0.0, NEG_INF).astype(np.float32))


def _window_attention(qr, qp, k, v, kx, vx, sink_rows, layer, tile):
    b, s, qw = qr.shape
    kw = k.shape[2]
    lx = kx.shape[1]
    assert kw == LANES and qw == 2 * LANES and tile % WINDOW == 0 and s >= 2 * WINDOW
    r = tile // WINDOW
    last = s // WINDOW - 1
    band = _window_band()
    qspec = pl.BlockSpec((None, tile, qw), lambda bi, i: (bi, i, 0))
    prev = pl.BlockSpec((None, WINDOW, kw), lambda bi, i: (bi, jnp.maximum(i * r - 1, 0), 0))
    cur = pl.BlockSpec((None, tile, kw), lambda bi, i: (bi, i, 0))
    nxt = pl.BlockSpec((None, WINDOW, kw), lambda bi, i: (bi, jnp.minimum((i + 1) * r, last), 0))
    ctx = pl.BlockSpec((None, lx, kw), lambda bi, i: (bi, 0, 0))
    return pl.pallas_call(
        _window_kernel,
        grid=(b, s // tile),
        in_specs=[qspec, qspec, prev, cur, nxt, prev, cur, nxt, ctx, ctx,
                  _slab(sink_rows.shape[1:], layer), _full(band.shape)],
        out_specs=pl.BlockSpec((None, qw, tile), lambda bi, i: (bi, 0, i)),
        out_shape=jax.ShapeDtypeStruct((b, qw, s), BF16),
        compiler_params=_params(2),
        name="window_attention",
    )(qr, qp, k, k, k, v, v, v, kx, vx, sink_rows, band)


NBR_MASKED = 2 * NA_ROWS - 1


def _nbr_kernel(q_ref, kp_ref, kc_ref, kn_ref, vp_ref, vc_ref, vn_ref, kx_ref, vx_ref, sl_ref, sr_ref,
                o_ref, bias_ref, *, slab_index):
    i = pl.program_id(1)
    nt = pl.num_programs(1)
    tq = kp_ref.shape[0]
    n_sub = q_ref.shape[0] // tq
    tile_rows = tq // GRID_W

    @pl.when((pl.program_id(0) == 0) & (i == 0))
    def _():
        for v in range(3):
            for g in range(q_ref.shape[1] // LANES):
                for bj in range(3 * tile_rows):
                    for e in range(2):
                        for pp in range(tile_rows // 2):
                            left = slab_index[v][2 * pp][bj // tile_rows][bj % tile_rows]
                            right = slab_index[v][2 * pp + 1][bj // tile_rows][bj % tile_rows]
                            bias_ref[v, g, bj * GRID_W:(bj + 1) * GRID_W,
                                     e * tq + pp * LANES:e * tq + (pp + 1) * LANES] = (
                                sl_ref[2 * g + e, left] + sr_ref[2 * g + e, right])

    kcat = jnp.concatenate([kp_ref[...], kc_ref[...], kn_ref[...]], axis=0)
    vcat = jnp.concatenate([vp_ref[...], vc_ref[...], vn_ref[...]], axis=0)
    kx = kx_ref[...]
    vx = vx_ref[...]
    for sub in range(n_sub):
        variant = 1
        if sub == 0:
            variant = jnp.where(i == 0, 0, variant)
        if sub == n_sub - 1:
            variant = jnp.where(i == nt - 1, 2, variant)
        q = q_ref[sub * tq:(sub + 1) * tq, :]
        rk = slice(sub * tq, (sub + 3) * tq)
        for g in range(q.shape[1] // LANES):
            cols = slice(g * LANES, (g + 1) * LANES)
            qg, vg, vxg = q[:, cols], vcat[rk, cols], vx[:, cols]
            low = _half_mask(qg.shape, True)
            zero = jnp.zeros_like(qg)
            qcat = jnp.concatenate([jnp.where(low, qg, zero), jnp.where(low, zero, qg)], axis=0)
            s_loc = _dot_nt(kcat[rk, cols], qcat) + bias_ref[variant, g]
            s_ctx = _dot_nt(kx[:, cols], qcat)
            for e in range(2):
                lower = e == 0
                qs = slice(e * tq, (e + 1) * tq)
                vw = jnp.where(_half_mask(vg.shape, lower), vg, jnp.ones_like(vg))
                vxw = jnp.where(_half_mask(vxg.shape, lower), vxg, jnp.ones_like(vxg))
                o = _softmax_pv_t([(s_loc[:, qs], vw), (s_ctx[:, qs], vxw)], None, lower)
                h = 2 * g + e
                o_ref[h * HEAD_DIM:(h + 1) * HEAD_DIM, sub * tq:(sub + 1) * tq] = o.astype(o_ref.dtype)


def _nbr_slab_index(rows, tile_rows):
    n_tiles = rows // tile_rows
    index = []
    for ti in (0, 1, n_tiles - 1):
        r = ti * tile_rows + np.arange(tile_rows)
        start = np.clip(r - NA_ROWS // 2, 0, rows - NA_ROWS)
        blk = np.arange(3)[:, None] - 1 + ti
        krow = blk * tile_rows + np.arange(tile_rows)[None, :]
        ok = ((krow[None] >= start[:, None, None]) & (krow[None] < start[:, None, None] + NA_ROWS)
              & ((blk >= 0) & (blk < n_tiles))[None])
        dr = krow[None] - r[:, None, None] + NA_ROWS - 1
        assert ((dr >= 0) & (dr < NBR_MASKED))[ok].all()
        index.append(np.where(ok, dr, NBR_MASKED))
    return np.stack(index).tolist()


def _nbr_slabs(rel_bias):
    qc = np.arange(GRID_W)[None, :]
    kc = np.arange(GRID_W)[:, None]
    dc = np.clip(kc - qc, 1 - NA_COLS, NA_COLS - 1) + NA_COLS - 1
    c_start = np.clip(qc - NA_COLS // 2, 0, GRID_W - NA_COLS)
    col_ok = (kc >= c_start) & (kc < c_start + NA_COLS)
    one_hot = jnp.asarray(np.eye(2 * NA_COLS - 1, dtype=np.float32)[dc])
    slabs = jnp.einsum('kqc,lhrc->lhrkq', one_hot, rel_bias.astype(F32), precision=lax.Precision.HIGHEST)
    slabs = jnp.where(jnp.asarray(col_ok), slabs, NEG_INF)
    slabs = jnp.concatenate([slabs, jnp.full_like(slabs[:, :, :1], NEG_INF)], axis=2)
    zeros = jnp.zeros_like(slabs)
    return jnp.concatenate([slabs, zeros], axis=-1), jnp.concatenate([zeros, slabs], axis=-1)


def _nbr_attention(q, k, v, kx, vx, slabs, slab_index, layer, tile, n_sub):
    b, s, w = q.shape
    lx = kx.shape[1]
    step = n_sub * tile
    nt = s // step
    last = s // tile - 1
    left, right = slabs
    tok = pl.BlockSpec((None, step, w), lambda bi, i: (bi, i, 0))
    prev = pl.BlockSpec((None, tile, w), lambda bi, i: (bi, jnp.maximum(i * n_sub - 1, 0), 0))
    nxt = pl.BlockSpec((None, tile, w), lambda bi, i: (bi, jnp.minimum((i + 1) * n_sub, last), 0))
    ctx = pl.BlockSpec((None, lx, w), lambda bi, i: (bi, 0, 0))
    return pl.pallas_call(
        functools.partial(_nbr_kernel, slab_index=slab_index),
        grid=(b, nt),
        in_specs=[tok, prev, tok, nxt, prev, tok, nxt, ctx, ctx,
                  _slab(left.shape[1:], layer), _slab(right.shape[1:], layer)],
        out_specs=pl.BlockSpec((None, w, step), lambda bi, i: (bi, 0, i)),
        out_shape=jax.ShapeDtypeStruct((b, w, s), BF16),
        scratch_shapes=[pltpu.VMEM((3, w // LANES, 3 * tile, 2 * tile), F32)],
        compiler_params=_params(2),
        name="nbr_attention",
    )(q, k, k, k, v, v, v, kx, vx, left, right)


def _ctx_attn_kernel(q_ref, k_ref, v_ref, sink_ref, o_ref, *, n_kv, group, use_sink, q_cols):
    q = q_ref[...]
    k = k_ref[...]
    v = v_ref[...]
    l = q.shape[0]
    outs = []
    for hk in range(n_kv):
        ks = slice(hk * HEAD_DIM, (hk + 1) * HEAD_DIM)
        for h in range(hk * group, (hk + 1) * group):
            s = _dot_nt(q[:, q_cols[h] * HEAD_DIM:(q_cols[h] + 1) * HEAD_DIM], k[:, ks])
            sink = jnp.broadcast_to(sink_ref[0:1, h:h + 1], (l, 1)) if use_sink else None
            outs.append(_softmax_pv([(s, v[:, ks])], sink))
    o_ref[...] = jnp.concatenate(outs, axis=1).astype(o_ref.dtype)


def _ctx_attention(q, k, v, sink, layer, use_sink, q_cols):
    b, l, qw = q.shape
    kw = k.shape[2]
    n_kv = kw // HEAD_DIM
    group = (qw // HEAD_DIM) // n_kv
    return pl.pallas_call(
        functools.partial(_ctx_attn_kernel, n_kv=n_kv, group=group, use_sink=use_sink, q_cols=q_cols),
        grid=(b,),
        in_specs=[pl.BlockSpec((None, l, qw), lambda bi: (bi, 0, 0)),
                  pl.BlockSpec((None, l, kw), lambda bi: (bi, 0, 0)),
                  pl.BlockSpec((None, l, kw), lambda bi: (bi, 0, 0)),
                  _slab(sink.shape[1:], layer)],
        out_specs=pl.BlockSpec((None, l, qw), lambda bi: (bi, 0, 0)),
        out_shape=jax.ShapeDtypeStruct((b, l, qw), BF16),
        compiler_params=_params(1),
        name="ctx_attention",
    )(q, k, v, sink)


def _merge_kernel(x_ref, mod_ref, g_ref, pp_ref, pc_ref, pn_ref, ab_ref, caw_ref,
                  yb_ref, hf_ref, hb_ref, cg_ref, yd_ref,
                  bg_ref, wbr_ref, wo_ref, *rest, sub):
    wgl_refs, o_ref = rest[:-1], rest[-1]
    i = pl.program_id(1)
    nt = pl.num_programs(1)
    t, d = x_ref.shape
    prev = jnp.where(i > 0, pp_ref[...], 0.0)
    nxt = jnp.where(i < nt - 1, pn_ref[...], 0.0)
    ext = jnp.concatenate([prev, pc_ref[...], nxt], axis=0)
    for r0 in range(0, t, sub):
        rows = slice(r0, r0 + sub)
        x = x_ref[rows, :]
        h = _adaln(x, g_ref[...], mod_ref[3:4, :], mod_ref[4:5, :]).astype(BF16)
        conv = (ext[HALO - 1 + r0:HALO - 1 + r0 + sub] * caw_ref[0:1, :]
                + ext[HALO + r0:HALO + r0 + sub] * caw_ref[1:2, :]
                + ext[HALO + 1 + r0:HALO + 1 + r0 + sub] * caw_ref[2:3, :])
        y_a = (ab_ref[rows, :] * conv).astype(BF16)
        y_c = ((hf_ref[rows, :] + hb_ref[rows, :]) * _gelu_tanh(cg_ref[rows, :])).astype(BF16)
        ys = ((y_a, _dot), (yb_ref[:, rows], _dot_tn), (y_c, _dot), (yd_ref[:, rows], _dot_tn))
        out = None
        for c0 in range(0, d, MERGE_CHUNK):
            merged = None
            for n, (y, dot) in enumerate(ys):
                gc = slice(n * d + c0, n * d + c0 + MERGE_CHUNK)
                gate = _sigmoid(_dot(h, wgl_refs[(n * d + c0) // MERGE_CHUNK][...]) + bg_ref[0:1, gc])
                term = gate * dot(y, wbr_ref[n, :, c0:c0 + MERGE_CHUNK])
                merged = term if merged is None else merged + term
            part = _dot(merged.astype(BF16), wo_ref[c0:c0 + MERGE_CHUNK, :])
            out = part if out is None else out + part
        o_ref[rows, :] = x + mod_ref[5:6, :] * out


def _merge(x, mods, norm_g, pa, ab, caw, yb, hf, hb, cg, yd, w_in, bg, wbr, wo, layer, ctx_row, tile, sub):
    b, s, d = x.shape
    mw = pa.shape[2]
    n_gate = N_BRANCH * d
    first, rem = divmod(w_in.shape[2] - n_gate, MERGE_CHUNK)
    assert rem == 0 and d % MERGE_CHUNK == 0
    gate_specs = [pl.BlockSpec((None, d, MERGE_CHUNK), lambda *_, k=k: (layer, 0, first + k),
                               pipeline_mode=pl.Buffered(1)) for k in range(n_gate // MERGE_CHUNK)]
    tok = lambda width: pl.BlockSpec((None, tile, width), lambda bi, i: (bi, i, 0))
    tok_t = pl.BlockSpec((None, mw, tile), lambda bi, i: (bi, 0, i))
    return pl.pallas_call(
        functools.partial(_merge_kernel, sub=min(sub, tile)),
        grid=(b, s // tile),
        in_specs=([tok(d), _mod_spec(d, layer, ctx_row), _slab((1, d), layer, 1)]
                  + _halo_specs(tile, mw, s, lambda i: i)
                  + [tok(mw), _slab((CONV_K, mw), layer),
                     tok_t, tok(mw), tok(mw), tok(mw), tok_t,
                     _slab(bg.shape[1:], layer), _slab(wbr.shape[1:], layer), _slab(wo.shape[1:], layer)]
                  + gate_specs),
        out_specs=tok(d),
        out_shape=jax.ShapeDtypeStruct((b, s, d), F32),
        compiler_params=_params(2),
        name="merge",
    )(x, mods, norm_g, pa, pa, pa, ab, caw, yb, hf, hb, cg, yd, bg, wbr, wo, *([w_in] * len(gate_specs)))


def _rope_tables(n_tok, width):
    half = HEAD_DIM // 2
    nf = half // 2
    inv_freq = ROPE_BASE ** (-jnp.arange(nf, dtype=F32) / nf)
    pos = jnp.arange(n_tok)
    ang_r = (pos // GRID_W).astype(F32)[:, None] * inv_freq[None, :]
    ang_c = (pos % GRID_W).astype(F32)[:, None] * inv_freq[None, :]
    cos = jnp.concatenate([jnp.cos(ang_r)] * 2 + [jnp.cos(ang_c)] * 2, axis=1)
    sin = jnp.concatenate([-jnp.sin(ang_r), jnp.sin(ang_r), -jnp.sin(ang_c), jnp.sin(ang_c)], axis=1)
    reps = width // HEAD_DIM
    return jnp.tile(cos, (1, reps)), jnp.tile(sin, (1, reps))


def _block_diag(w):
    *lead, n, c, e = w.shape
    eye = jnp.eye(n, dtype=w.dtype)
    return (eye[:, None, :, None] * w[..., :, :, None, :]).reshape(*lead, n * c, n * e)


def _pick_tile(n, target):
    t = min(n, target)
    while n % t:
        t //= 2
    return t


def kernel(x, c, ctx, c_ctx, w_mod, b_mod, norm_g, ffn_w_gate, ffn_w_up, ffn_w_down, w_in, b_gate,
           conv_a_w, qk_norm_g, attn_sink, lru_conv_w, lru_conv_b, lru_w_a, lru_b_a, lru_w_x, lru_b_x,
           lru_lam, na_rel_bias, w_branch, w_out):
    b, s, d = x.shape
    lx = ctx.shape[1]
    depth = w_mod.shape[0]
    mw = d // 4
    kvw = mw // 2
    rows = s // GRID_W
    assert s % GRID_W == 0 and rows >= 4 * NA_ROWS and lx % HALO == 0

    tile = _pick_tile(s, 512)
    tile_big = _pick_tile(s, 1024)
    tile_x = _pick_tile(lx, 512)
    tile_cf = _pick_tile(b * lx, 1024)
    tile_win = _pick_tile(s, 1024)
    nbr_rows = 4
    tile_nbr = nbr_rows * GRID_W
    nbr_sub = max(1, _pick_tile(s, 1024) // tile_nbr)

    assert b + 1 <= 8
    cs_t = jnp.zeros((d, 8), F32).at[:, :b].set(c.T).at[:, b].set(c_ctx)
    mods = _modulation(cs_t, b + 1, w_mod, b_mod).reshape(depth, 8, N_MOD, d)
    lat, cx_row = None, b

    cos, sin = _rope_tables(s, kvw)
    cos_x, sin_x = jnp.ones((lx, kvw), F32), jnp.zeros((lx, kvw), F32)
    lane = np.arange(mw)
    mhead = jnp.asarray((lane[:, None] // HEAD_DIM == lane[None, :] // HEAD_DIM) / HEAD_DIM, dtype=BF16)

    ng = norm_g.reshape(depth, 3, 1, d)
    wg, wu, wd = ffn_w_gate.astype(BF16), ffn_w_up.astype(BF16), ffn_w_down.astype(BF16)
    w_in_b = w_in.astype(BF16)
    qkg = jnp.tile(qk_norm_g, (1, 1, mw // HEAD_DIM))
    sink = (attn_sink * LOG2E).reshape(depth, 1, -1)
    sink_rows = jnp.repeat(sink.reshape(depth, 2, 1, 2), WINDOW, axis=3)
    lru_par = (lru_conv_w, lru_conv_b.reshape(depth, 1, mw), _block_diag(lru_w_a).astype(BF16), lru_b_a,
               _block_diag(lru_w_x).astype(BF16), lru_b_x, lru_lam)
    slabs = _nbr_slabs(na_rel_bias * LOG2E)
    slab_index = _nbr_slab_index(rows, nbr_rows)
    bg = b_gate.reshape(depth, 1, -1)
    wbr = w_branch.astype(BF16)
    wo = w_out.astype(BF16)

    xc = ctx
    for l in range(depth):
        ctx_out = l < depth - 1
        x = _ffn(x, mods, ng, wg, wu, wd, l, 0, lat, tile_big, tile)
        xc = _ffn(xc.reshape(1, b * lx, d), mods, ng, wg, wu, wd, l, 0, cx_row,
                  tile_cf, tile).reshape(b, lx, d)
        (pa, ab, qr, qp, kr, vb, cxl, cg, dq, dk, dv) = _inproj(
            x, mods, ng, w_in_b, qkg, mhead, cos, sin, l, lat, tile_big, tile)
        (pa_c, ab_c, _, qp_c, k_c, vb_c, cx_c, cg_c, dq_c, dk_c, dv_c) = _inproj(
            xc, mods, ng, w_in_b, qkg, mhead, cos_x, sin_x, l, cx_row, tile_x, tile_x)

        hf_c, hb_c = _scan(cx_c, jnp.zeros((b, 2, mw), F32), *lru_par, l, tile_x)
        h0 = jnp.stack([hf_c[:, lx - 1], hb_c[:, 0]], axis=1)
        hf, hb = _scan(cxl, h0, *lru_par, l, tile_big)
        yb = _window_attention(qr, qp, kr, vb, k_c, vb_c, sink_rows, l, tile_win)
        yd = _nbr_attention(dq, dk, dv, dk_c, dv_c, slabs, slab_index, l, tile_nbr, nbr_sub)
        x = _merge(x, mods, ng, pa, ab, conv_a_w, yb, hf, hb, cg, yd, w_in_b, bg, wbr, wo, l, lat, tile_big, tile)
        x = _ffn(x, mods, ng, wg, wu, wd, l, 1, lat, tile_big, tile)
        if ctx_out:
            yb_c = jnp.swapaxes(_ctx_attention(qp_c, k_c, vb_c, sink, l, True, WIN_Q_COLS), 1, 2)
            yd_c = jnp.swapaxes(_ctx_attention(dq_c, dk_c, dv_c, sink, l, False, (0, 1, 2, 3)), 1, 2)
            xc = _merge(xc, mods, ng, pa_c, ab_c, conv_a_w, yb_c, hf_c, hb_c, cg_c, yd_c,
                        w_in_b, bg, wbr, wo, l, cx_row, tile_x, tile_x)
            xc = _ffn(xc.reshape(1, b * lx, d), mods, ng, wg, wu, wd, l, 1, cx_row,
                      tile_cf, tile).reshape(b, lx, d)
    return x
```

```python
import functools

import jax
import jax.numpy as jnp
import numpy as np
from jax import lax
from jax.experimental import pallas as pl
from jax.experimental.pallas import tpu as pltpu

HEAD_DIM = 64
GRID_W = 64
WINDOW = 128
NA_ROWS = 8
NA_COLS = 16
N_BRANCH = 4
CONV_K = 3
LRU_CONV_K = 4
LRU_C = 8.0
ROPE_BASE = 10000.0
EPS = 1e-6
NEG_INF = -1e30
N_MOD = 9
LOG2E = 1.4426950408889634

V7X_VMEM_LIMIT_BYTES = 60 * 1024 * 1024
HALO = 8
LANES = 128
MERGE_CHUNK = 512

F32 = jnp.float32
BF16 = jnp.bfloat16


def _params(n_grid):
    return pltpu.CompilerParams(
        dimension_semantics=("arbitrary",) * n_grid,
        vmem_limit_bytes=V7X_VMEM_LIMIT_BYTES)


def _dot(a, b):
    return jnp.dot(a, b, preferred_element_type=F32)


def _dot_nt(a, b):
    return lax.dot_general(a, b, (((1,), (1,)), ((), ())), preferred_element_type=F32)


def _sigmoid(x):
    return jax.nn.sigmoid(x)


def _gelu_tanh(x):
    return 0.5 * x * (1.0 + jnp.tanh(0.7978845608028654 * (x + 0.044715 * (x * x * x))))


def _adaln(x, g, shift, scale):
    ms = jnp.mean(x * x, axis=-1, keepdims=True)
    return (x * lax.rsqrt(ms + EPS) * g) * (1.0 + scale) + shift


def _full(shape):
    n = len(shape)
    return pl.BlockSpec(shape, lambda *_: (0,) * n)


def _slab(shape, *lead):
    n = len(shape)
    return pl.BlockSpec((None,) * len(lead) + tuple(shape), lambda *_: tuple(lead) + (0,) * n,
                        pipeline_mode=pl.Buffered(1))


def _mod_spec(d, layer, ctx_row):
    if ctx_row is None:
        return pl.BlockSpec((None, None, N_MOD, d), lambda bi, i: (layer, bi, 0, 0))
    return pl.BlockSpec((None, None, N_MOD, d), lambda bi, i: (layer, ctx_row, 0, 0))


def _mod_kernel(ct_ref, w_ref, b_ref, o_ref, *, n_rows):
    ct = ct_ref[...]
    st = ct * _sigmoid(ct)
    w = w_ref[...]
    rows = [jnp.sum(w * st[:, r:r + 1], axis=0, keepdims=True) for r in range(n_rows)]
    rows += [jnp.zeros_like(rows[0])] * (o_ref.shape[0] - n_rows)
    o_ref[...] = jnp.concatenate(rows, axis=0) + b_ref[...]


def _modulation(cs_t, n_rows, w_mod, b_mod):
    depth, d, n = w_mod.shape
    tn = n // 8
    return pl.pallas_call(
        functools.partial(_mod_kernel, n_rows=n_rows),
        grid=(depth, n // tn),
        in_specs=[pl.BlockSpec((d, 8), lambda l, j: (0, 0)),
                  pl.BlockSpec((None, d, tn), lambda l, j: (l, 0, j)),
                  pl.BlockSpec((None, 1, tn), lambda l, j: (l, 0, j))],
        out_specs=pl.BlockSpec((None, 8, tn), lambda l, j: (l, 0, j)),
        out_shape=jax.ShapeDtypeStruct((depth, 8, n), F32),
        compiler_params=_params(2),
        name="modulation",
    )(cs_t, w_mod, b_mod.reshape(depth, 1, n))


def _ffn_kernel(x_ref, mod_ref, g_ref, wg_ref, wu_ref, wd_ref, o_ref, *, row0, chunks, sub):
    for r0 in range(0, x_ref.shape[0], sub):
        rows = slice(r0, r0 + sub)
        x = x_ref[rows, :]
        h = _adaln(x, g_ref[...], mod_ref[row0:row0 + 1, :], mod_ref[row0 + 1:row0 + 2, :]).astype(BF16)
        acc = None
        for (lo, hi) in chunks:
            a = _dot(h, wg_ref[:, lo:hi])
            u = _dot(h, wu_ref[:, lo:hi])
            act = (a * _sigmoid(a) * u).astype(BF16)
            part = _dot(act, wd_ref[lo:hi, :])
            acc = part if acc is None else acc + part
        o_ref[rows, :] = x + (0.5 * mod_ref[row0 + 2:row0 + 3, :]) * acc


def _ffn(x, mods, norm_g, wg, wu, wd, layer, which, ctx_row, tile, sub):
    b, s, d = x.shape
    f = wg.shape[3]
    half = (f // 2 + 255) // 256 * 256
    chunks = ((0, half), (half, f)) if half < f else ((0, f),)
    return pl.pallas_call(
        functools.partial(_ffn_kernel, row0=6 * which, chunks=chunks, sub=min(sub, tile)),
        grid=(b, s // tile),
        in_specs=[pl.BlockSpec((None, tile, d), lambda bi, i: (bi, i, 0)),
                  _mod_spec(d, layer, ctx_row),
                  _slab((1, d), layer, 2 * which),
                  _slab((d, f), layer, which), _slab((d, f), layer, which), _slab((f, d), layer, which)],
        out_specs=pl.BlockSpec((None, tile, d), lambda bi, i: (bi, i, 0)),
        out_shape=jax.ShapeDtypeStruct((b, s, d), F32),
        compiler_params=_params(2),
        name="ffn",
    )(x, mods, norm_g, wg, wu, wd)


def _head_rms(t, m_ref, g):
    n = t.shape[1]
    sq = t * t
    hi = sq.astype(BF16)
    lo = (sq - hi.astype(F32)).astype(BF16)
    m = m_ref[0:n, 0:n]
    ms = _dot(hi, m) + _dot(lo, m)
    return t * lax.rsqrt(ms + EPS) * g


def _rope(t, cos, sin_signed):
    n = t.shape[1]
    lane = lax.broadcasted_iota(jnp.int32, t.shape, 1)
    partner = jnp.where((lane % 32) < 16, pltpu.roll(t, n - 16, 1), pltpu.roll(t, 16, 1))
    return t * cos + partner * sin_signed


def _inproj_kernel(x_ref, mod_ref, g_ref, w_ref, qkg_ref, m_ref, cos_ref, sin_ref,
                   pa_ref, ab_ref, qr_ref, qp_ref, kr_ref, vb_ref, cx_ref, cg_ref,
                   dq_ref, dk_ref, dv_ref, *, mw, sub):
    kvw = mw // 2
    scale = LOG2E * HEAD_DIM ** -0.5
    for r0 in range(0, x_ref.shape[0], sub):
        rows = slice(r0, r0 + sub)
        h = _adaln(x_ref[rows, :], g_ref[...], mod_ref[3:4, :], mod_ref[4:5, :]).astype(BF16)
        u = _dot(h, w_ref[...])
        o = 0
        ax = u[:, o:o + mw]; o += mw
        ab = u[:, o:o + mw]; o += mw
        ac = u[:, o:o + mw]; o += mw
        bq = u[:, o:o + mw]; o += mw
        qa, qb = bq[:, :LANES], bq[:, LANES:]
        low = _half_mask(qa.shape, True)
        bq = jnp.concatenate([jnp.where(low, qa, pltpu.roll(qb, HEAD_DIM, 1)),
                              jnp.where(low, pltpu.roll(qa, HEAD_DIM, 1), qb)], axis=1)
        bk = u[:, o:o + kvw]; o += kvw
        bv = u[:, o:o + kvw]; o += kvw
        cx = u[:, o:o + mw]; o += mw
        cg = u[:, o:o + mw]; o += mw
        dq = u[:, o:o + mw]; o += mw
        dk = u[:, o:o + mw]; o += mw
        dv = u[:, o:o + mw]; o += mw
        pa_ref[rows, :] = ac * ax
        ab_ref[rows, :] = ab
        cx_ref[rows, :] = cx
        cg_ref[rows, :] = cg
        cos = cos_ref[rows, :]
        sin = sin_ref[rows, :]
        cos2 = jnp.concatenate([cos, cos], axis=1)
        sin2 = jnp.concatenate([sin, sin], axis=1)
        q = _head_rms(bq, m_ref, qkg_ref[0:1, :]) * scale
        qp_ref[rows, :] = q.astype(BF16)
        qr_ref[rows, :] = _rope(q, cos2, sin2).astype(BF16)
        k = _head_rms(bk, m_ref, qkg_ref[1:2, 0:kvw])
        kr_ref[rows, :] = _rope(k, cos, sin).astype(BF16)
        vb_ref[rows, :] = bv.astype(BF16)
        dq_ref[rows, :] = (_head_rms(dq, m_ref, qkg_ref[2:3, :]) * scale).astype(BF16)
        dk_ref[rows, :] = _head_rms(dk, m_ref, qkg_ref[3:4, :]).astype(BF16)
        dv_ref[rows, :] = dv.astype(BF16)


def _inproj(x, mods, norm_g, w_in, qkg, mhead, cos, sin, layer, ctx_row, tile, sub):
    b, s, d = x.shape
    mw = qkg.shape[2]
    kvw = mw // 2
    nw = 9 * mw + 2 * kvw
    assert mw == 2 * LANES
    tok = lambda width: pl.BlockSpec((None, tile, width), lambda bi, i: (bi, i, 0))
    widths = (mw, mw, mw, mw, kvw, kvw, mw, mw, mw, mw, mw)
    dtypes = (F32, F32, BF16, BF16, BF16, BF16, F32, F32, BF16, BF16, BF16)
    return pl.pallas_call(
        functools.partial(_inproj_kernel, mw=mw, sub=min(sub, tile)),
        grid=(b, s // tile),
        in_specs=[tok(d),
                  _mod_spec(d, layer, ctx_row),
                  _slab((1, d), layer, 1), _slab((d, nw), layer), _slab((4, mw), layer), _full((mw, mw)),
                  pl.BlockSpec((tile, kvw), lambda bi, i: (i, 0)),
                  pl.BlockSpec((tile, kvw), lambda bi, i: (i, 0))],
        out_specs=[tok(wd) for wd in widths],
        out_shape=[jax.ShapeDtypeStruct((b, s, wd), dt) for wd, dt in zip(widths, dtypes)],
        compiler_params=_params(2),
        name="inproj",
    )(x, mods, norm_g, w_in, qkg, mhead, cos, sin)


def _shift_rows(t, d, fill, reverse):
    n = t.shape[0]
    row = lax.broadcasted_iota(jnp.int32, t.shape, 0)
    if reverse:
        return jnp.where(row < n - d, pltpu.roll(t, n - d, 0), fill)
    return jnp.where(row >= d, pltpu.roll(t, d, 0), fill)


def _lru_coeffs(prev_ref, cur_ref, next_ref, has_prev, has_next, cw_ref, cb_ref,
                wa_ref, ba_ref, wx_ref, bx_ref, lam_ref, direction):
    cur = cur_ref[...]
    t = cur.shape[0]
    prev = jnp.where(has_prev, prev_ref[...], 0.0)
    nxt = jnp.where(has_next, next_ref[...], 0.0)
    ext = jnp.concatenate([prev, cur, nxt], axis=0)
    u = cb_ref[...] + cur * cw_ref[2:3, :]
    u = u + ext[HALO - 2:HALO - 2 + t] * cw_ref[0:1, :]
    u = u + ext[HALO - 1:HALO - 1 + t] * cw_ref[1:2, :]
    u = u + ext[HALO + 1:HALO + 1 + t] * cw_ref[3:4, :]
    ub = u.astype(BF16)
    r = _sigmoid(_dot(ub, wa_ref[direction]) + ba_ref[direction:direction + 1, :])
    ig = _sigmoid(_dot(ub, wx_ref[direction]) + bx_ref[direction:direction + 1, :])
    nl = -lam_ref[direction:direction + 1, :]
    softplus = jnp.maximum(nl, 0.0) + jnp.log1p(jnp.exp(-jnp.abs(nl)))
    neg_log_a = r * (LRU_C * softplus)
    a = jnp.exp2(r * ((-LRU_C * LOG2E) * softplus))
    z = jnp.tanh(neg_log_a) * (a * a + 1.0)
    return a, jnp.where(z > 0.0, z * lax.rsqrt(z), 0.0) * (ig * u)


def _scan_tile(a, bb, carry, reverse, sa_ref, sb_ref, sc_ref, out_ref):
    t, w = a.shape
    g = t // HALO
    a3 = a.reshape(g, HALO, w)
    b3 = bb.reshape(g, HALO, w)
    sub = lax.broadcasted_iota(jnp.int32, (g, HALO, w), 1)
    d = 1
    while d < HALO:
        ok = (sub < HALO - d) if reverse else (sub >= d)
        shift = HALO - d if reverse else d
        a_r = pltpu.roll(a3, shift, 1)
        b_r = pltpu.roll(b3, shift, 1)
        b3 = jnp.where(ok, a3 * b_r + b3, b3)
        a3 = jnp.where(ok, a3 * a_r, a3)
        d *= 2
    sa_ref[...] = a3.reshape(t, w)
    sb_ref[...] = b3.reshape(t, w)
    edge = 0 if reverse else HALO - 1
    at = sa_ref[pl.ds(edge, g, stride=HALO), :]
    bt = sb_ref[pl.ds(edge, g, stride=HALO), :]
    d = 1
    while d < g:
        a_s = _shift_rows(at, d, 1.0, reverse)
        b_s = _shift_rows(bt, d, 0.0, reverse)
        bt = at * b_s + bt
        at = at * a_s
        d *= 2
    h_end = at * carry + bt
    sc_ref[...] = _shift_rows(h_end, 1, carry, reverse)
    for gi in range(g):
        rows = slice(gi * HALO, (gi + 1) * HALO)
        out_ref[rows, :] = sa_ref[rows, :] * sc_ref[gi:gi + 1, :] + sb_ref[rows, :]
    return h_end[0:1, :] if reverse else h_end[g - 1:g, :]


def _scan_kernel(fp_ref, fc_ref, fn_ref, rp_ref, rc_ref, rn_ref, h0_ref, cw_ref, cb_ref,
                 wa_ref, ba_ref, wx_ref, bx_ref, lam_ref, hf_ref, hb_ref,
                 carry_ref, sa_ref, sb_ref, sc_ref):
    i = pl.program_id(1)
    nt = pl.num_programs(1)

    @pl.when(i == 0)
    def _():
        carry_ref[...] = h0_ref[...]

    par = (cw_ref, cb_ref, wa_ref, ba_ref, wx_ref, bx_ref, lam_ref)
    tiles = ((fp_ref, fc_ref, fn_ref, i > 0, i < nt - 1, hf_ref),
             (rp_ref, rc_ref, rn_ref, i < nt - 1, i > 0, hb_ref))
    for direction, (p_ref, c_ref, n_ref, has_prev, has_next, out_ref) in enumerate(tiles):
        a, bb = _lru_coeffs(p_ref, c_ref, n_ref, has_prev, has_next, *par, direction)
        for j in range(a.shape[1] // LANES):
            cols = slice(j * LANES, (j + 1) * LANES)
            carry_ref[direction:direction + 1, cols] = _scan_tile(
                a[:, cols], bb[:, cols], carry_ref[direction:direction + 1, cols], direction == 1,
                sa_ref.at[direction, j], sb_ref.at[direction, j], sc_ref.at[direction, j],
                out_ref.at[:, cols])


def _halo_specs(tile, width, n_rows, tile_index):
    r = tile // HALO
    last = n_rows // HALO - 1
    prev = pl.BlockSpec((None, HALO, width),
                        lambda bi, i: (bi, jnp.maximum(tile_index(i) * r - 1, 0), 0))
    cur = pl.BlockSpec((None, tile, width), lambda bi, i: (bi, tile_index(i), 0))
    nxt = pl.BlockSpec((None, HALO, width),
                       lambda bi, i: (bi, jnp.minimum((tile_index(i) + 1) * r, last), 0))
    return [prev, cur, nxt]


def _scan(cx, h0, cw, cb, wa, ba, wx, bx, lam, layer, tile):
    b, s, w = cx.shape
    nt = s // tile
    fwd = lambda i: i
    rev = lambda i: nt - 1 - i
    return pl.pallas_call(
        _scan_kernel,
        grid=(b, nt),
        in_specs=(_halo_specs(tile, w, s, fwd) + _halo_specs(tile, w, s, rev)
                  + [pl.BlockSpec((None, 2, w), lambda bi, i: (bi, 0, 0)),
                     _slab((LRU_CONV_K, w), layer), _slab((1, w), layer),
                     _slab((2, w, w), layer), _slab((2, w), layer),
                     _slab((2, w, w), layer), _slab((2, w), layer), _slab((2, w), layer)]),
        out_specs=[pl.BlockSpec((None, tile, w), lambda bi, i: (bi, i, 0)),
                   pl.BlockSpec((None, tile, w), lambda bi, i: (bi, nt - 1 - i, 0))],
        out_shape=[jax.ShapeDtypeStruct((b, s, w), F32)] * 2,
        scratch_shapes=[pltpu.VMEM((2, w), F32),
                        pltpu.VMEM((2, w // LANES, tile, LANES), F32),
                        pltpu.VMEM((2, w // LANES, tile, LANES), F32),
                        pltpu.VMEM((2, w // LANES, tile // HALO, LANES), F32)],
        compiler_params=_params(2),
        name="lru_scan",
    )(cx, cx, cx, cx, cx, cx, h0, cw, cb, wa, ba, wx, bx, lam)


def _softmax_pv(parts, sink):
    m = None
    for s, _ in parts:
        mx = jnp.max(s, axis=-1, keepdims=True)
        m = mx if m is None else jnp.maximum(m, mx)
    if sink is not None:
        m = jnp.maximum(m, sink)
    den = None
    out = None
    for s, v in parts:
        p = jnp.exp2(s - m)
        sm = jnp.sum(p, axis=-1, keepdims=True)
        den = sm if den is None else den + sm
        pv = _dot(p.astype(BF16), v)
        out = pv if out is None else out + pv
    if sink is not None:
        den = den + jnp.exp2(sink - m)
    return out / den


def _dot_tn(a, b):
    return lax.dot_general(a, b, (((0,), (0,)), ((), ())), preferred_element_type=F32)


def _half_mask(shape, lower):
    lane = lax.broadcasted_iota(jnp.int32, shape, 1) % LANES
    return (lane < HEAD_DIM) if lower else (lane >= HEAD_DIM)


def _softmax_pv_t(parts, sink_row, lower):
    m = None
    for s, _ in parts:
        mx = jnp.max(s, axis=0, keepdims=True)
        m = mx if m is None else jnp.maximum(m, mx)
    if sink_row is not None:
        m = jnp.maximum(m, sink_row)
    res = None
    for s, v in parts:
        r = _dot_tn(v, jnp.exp2(s - m).astype(BF16))
        res = r if res is None else res + r
    out, den = (res[:HEAD_DIM], res[HEAD_DIM:HEAD_DIM + 1]) if lower else (res[HEAD_DIM:], res[0:1])
    if sink_row is not None:
        den = den + jnp.exp2(sink_row - m)
    return out * (1.0 / den)


WIN_Q_COLS = (0, 2, 1, 3)


def _window_kernel(qr_ref, qp_ref, kp_ref, kc_ref, kn_ref, vp_ref, vc_ref, vn_ref,
                   kx_ref, vx_ref, sink_ref, band_ref, o_ref):
    i = pl.program_id(1)
    nt = pl.num_programs(1)
    tq = qr_ref.shape[0]
    nblk = tq // WINDOW
    qr = qr_ref[...]
    qp = qp_ref[...]
    kcat = jnp.concatenate([kp_ref[...], kc_ref[...], kn_ref[...]], axis=0)
    vcat = jnp.concatenate([vp_ref[...], vc_ref[...], vn_ref[...]], axis=0)
    kx = kx_ref[...]
    vx = vx_ref[...]
    for hk in range(2):
        lower = hk == 0
        qm = _half_mask((tq, LANES), lower)
        zero = jnp.zeros((tq, LANES), BF16)
        qra, qrb = jnp.where(qm, qr[:, :LANES], zero), jnp.where(qm, qr[:, LANES:], zero)
        qpa, qpb = jnp.where(qm, qp[:, :LANES], zero), jnp.where(qm, qp[:, LANES:], zero)
        vw = jnp.where(_half_mask(vcat.shape, lower), vcat, jnp.ones_like(vcat))
        vxw = jnp.where(_half_mask(vx.shape, lower), vx, jnp.ones_like(vx))
        s_ctx = _dot_nt(kx, jnp.concatenate([qpa, qpb], axis=0))
        for j in range(nblk):
            rq = slice(j * WINDOW, (j + 1) * WINDOW)
            rk = slice(j * WINDOW, (j + 3) * WINDOW)
            if j == 0:
                band = band_ref[jnp.where(i == 0, 1, 0)]
            elif j == nblk - 1:
                band = band_ref[jnp.where(i == nt - 1, 2, 0)]
            else:
                band = band_ref[0]
            s_loc = (_dot_nt(kcat[rk], jnp.concatenate([qra[rq], qrb[rq]], axis=0))
                     + jnp.concatenate([band, band], axis=1))
            s_cx = jnp.concatenate([s_ctx[:, rq], s_ctx[:, tq + j * WINDOW:tq + (j + 1) * WINDOW]], axis=1)
            o = _softmax_pv_t([(s_loc, vw[rk]), (s_cx, vxw)], sink_ref[hk], lower).astype(o_ref.dtype)
            ha, hb = 2 * hk, 2 * hk + 1
            o_ref[ha * HEAD_DIM:(ha + 1) * HEAD_DIM, rq] = o[:, :WINDOW]
            o_ref[hb * HEAD_DIM:(hb + 1) * HEAD_DIM, rq] = o[:, WINDOW:]


def _window_band():
    kk = np.arange(3 * WINDOW)[:, None]
    r = np.arange(WINDOW)[None, :]
    ok = np.abs(kk - WINDOW - r) <= WINDOW
    variants = [ok, ok & (kk >= WINDOW), ok & (kk < 2 * WINDOW)]
    return jnp.asarray(np.where(np.stack(variants), 0.0, NEG_INF).astype(np.float32))


def _window_attention(qr, qp, k, v, kx, vx, sink_rows, layer, tile):
    b, s, qw = qr.shape
    kw = k.shape[2]
    lx = kx.shape[1]
    assert kw == LANES and qw == 2 * LANES and tile % WINDOW == 0 and s >= 2 * WINDOW
    r = tile // WINDOW
    last = s // WINDOW - 1
    band = _window_band()
    qspec = pl.BlockSpec((None, tile, qw), lambda bi, i: (bi, i, 0))
    prev = pl.BlockSpec((None, WINDOW, kw), lambda bi, i: (bi, jnp.maximum(i * r - 1, 0), 0))
    cur = pl.BlockSpec((None, tile, kw), lambda bi, i: (bi, i, 0))
    nxt = pl.BlockSpec((None, WINDOW, kw), lambda bi, i: (bi, jnp.minimum((i + 1) * r, last), 0))
    ctx = pl.BlockSpec((None, lx, kw), lambda bi, i: (bi, 0, 0))
    return pl.pallas_call(
        _window_kernel,
        grid=(b, s // tile),
        in_specs=[qspec, qspec, prev, cur, nxt, prev, cur, nxt, ctx, ctx,
                  _slab(sink_rows.shape[1:], layer), _full(band.shape)],
        out_specs=pl.BlockSpec((None, qw, tile), lambda bi, i: (bi, 0, i)),
        out_shape=jax.ShapeDtypeStruct((b, qw, s), BF16),
        compiler_params=_params(2),
        name="window_attention",
    )(qr, qp, k, k, k, v, v, v, kx, vx, sink_rows, band)


NBR_MASKED = 2 * NA_ROWS - 1


def _nbr_kernel(q_ref, kp_ref, kc_ref, kn_ref, vp_ref, vc_ref, vn_ref, kx_ref, vx_ref, sl_ref, sr_ref,
                o_ref, bias_ref, *, slab_index):
    i = pl.program_id(1)
    nt = pl.num_programs(1)
    tq = kp_ref.shape[0]
    n_sub = q_ref.shape[0] // tq
    tile_rows = tq // GRID_W

    @pl.when((pl.program_id(0) == 0) & (i == 0))
    def _():
        for v in range(3):
            for g in range(q_ref.shape[1] // LANES):
                for bj in range(3 * tile_rows):
                    for e in range(2):
                        for pp in range(tile_rows // 2):
                            left = slab_index[v][2 * pp][bj // tile_rows][bj % tile_rows]
                            right = slab_index[v][2 * pp + 1][bj // tile_rows][bj % tile_rows]
                            bias_ref[v, g, bj * GRID_W:(bj + 1) * GRID_W,
                                     e * tq + pp * LANES:e * tq + (pp + 1) * LANES] = (
                                sl_ref[2 * g + e, left] + sr_ref[2 * g + e, right])

    kcat = jnp.concatenate([kp_ref[...], kc_ref[...], kn_ref[...]], axis=0)
    vcat = jnp.concatenate([vp_ref[...], vc_ref[...], vn_ref[...]], axis=0)
    kx = kx_ref[...]
    vx = vx_ref[...]
    for sub in range(n_sub):
        variant = 1
        if sub == 0:
            variant = jnp.where(i == 0, 0, variant)
        if sub == n_sub - 1:
            variant = jnp.where(i == nt - 1, 2, variant)
        q = q_ref[sub * tq:(sub + 1) * tq, :]
        rk = slice(sub * tq, (sub + 3) * tq)
        for g in range(q.shape[1] // LANES):
            cols = slice(g * LANES, (g + 1) * LANES)
            qg, vg, vxg = q[:, cols], vcat[rk, cols], vx[:, cols]
            low = _half_mask(qg.shape, True)
            zero = jnp.zeros_like(qg)
            qcat = jnp.concatenate([jnp.where(low, qg, zero), jnp.where(low, zero, qg)], axis=0)
            s_loc = _dot_nt(kcat[rk, cols], qcat) + bias_ref[variant, g]
            s_ctx = _dot_nt(kx[:, cols], qcat)
            for e in range(2):
                lower = e == 0
                qs = slice(e * tq, (e + 1) * tq)
                vw = jnp.where(_half_mask(vg.shape, lower), vg, jnp.ones_like(vg))
                vxw = jnp.where(_half_mask(vxg.shape, lower), vxg, jnp.ones_like(vxg))
                o = _softmax_pv_t([(s_loc[:, qs], vw), (s_ctx[:, qs], vxw)], None, lower)
                h = 2 * g + e
                o_ref[h * HEAD_DIM:(h + 1) * HEAD_DIM, sub * tq:(sub + 1) * tq] = o.astype(o_ref.dtype)


def _nbr_slab_index(rows, tile_rows):
    n_tiles = rows // tile_rows
    index = []
    for ti in (0, 1, n_tiles - 1):
        r = ti * tile_rows + np.arange(tile_rows)
        start = np.clip(r - NA_ROWS // 2, 0, rows - NA_ROWS)
        blk = np.arange(3)[:, None] - 1 + ti
        krow = blk * tile_rows + np.arange(tile_rows)[None, :]
        ok = ((krow[None] >= start[:, None, None]) & (krow[None] < start[:, None, None] + NA_ROWS)
              & ((blk >= 0) & (blk < n_tiles))[None])
        dr = krow[None] - r[:, None, None] + NA_ROWS - 1
        assert ((dr >= 0) & (dr < NBR_MASKED))[ok].all()
        index.append(np.where(ok, dr, NBR_MASKED))
    return np.stack(index).tolist()


def _nbr_slabs(rel_bias):
    qc = np.arange(GRID_W)[None, :]
    kc = np.arange(GRID_W)[:, None]
    dc = np.clip(kc - qc, 1 - NA_COLS, NA_COLS - 1) + NA_COLS - 1
    c_start = np.clip(qc - NA_COLS // 2, 0, GRID_W - NA_COLS)
    col_ok = (kc >= c_start) & (kc < c_start + NA_COLS)
    one_hot = jnp.asarray(np.eye(2 * NA_COLS - 1, dtype=np.float32)[dc])
    slabs = jnp.einsum('kqc,lhrc->lhrkq', one_hot, rel_bias.astype(F32), precision=lax.Precision.HIGHEST)
    slabs = jnp.where(jnp.asarray(col_ok), slabs, NEG_INF)
    slabs = jnp.concatenate([slabs, jnp.full_like(slabs[:, :, :1], NEG_INF)], axis=2)
    zeros = jnp.zeros_like(slabs)
    return jnp.concatenate([slabs, zeros], axis=-1), jnp.concatenate([zeros, slabs], axis=-1)


def _nbr_attention(q, k, v, kx, vx, slabs, slab_index, layer, tile, n_sub):
    b, s, w = q.shape
    lx = kx.shape[1]
    step = n_sub * tile
    nt = s // step
    last = s // tile - 1
    left, right = slabs
    tok = pl.BlockSpec((None, step, w), lambda bi, i: (bi, i, 0))
    prev = pl.BlockSpec((None, tile, w), lambda bi, i: (bi, jnp.maximum(i * n_sub - 1, 0), 0))
    nxt = pl.BlockSpec((None, tile, w), lambda bi, i: (bi, jnp.minimum((i + 1) * n_sub, last), 0))
    ctx = pl.BlockSpec((None, lx, w), lambda bi, i: (bi, 0, 0))
    return pl.pallas_call(
        functools.partial(_nbr_kernel, slab_index=slab_index),
        grid=(b, nt),
        in_specs=[tok, prev, tok, nxt, prev, tok, nxt, ctx, ctx,
                  _slab(left.shape[1:], layer), _slab(right.shape[1:], layer)],
        out_specs=pl.BlockSpec((None, w, step), lambda bi, i: (bi, 0, i)),
        out_shape=jax.ShapeDtypeStruct((b, w, s), BF16),
        scratch_shapes=[pltpu.VMEM((3, w // LANES, 3 * tile, 2 * tile), F32)],
        compiler_params=_params(2),
        name="nbr_attention",
    )(q, k, k, k, v, v, v, kx, vx, left, right)


def _ctx_attn_kernel(q_ref, k_ref, v_ref, sink_ref, o_ref, *, n_kv, group, use_sink, q_cols):
    q = q_ref[...]
    k = k_ref[...]
    v = v_ref[...]
    l = q.shape[0]
    outs = []
    for hk in range(n_kv):
        ks = slice(hk * HEAD_DIM, (hk + 1) * HEAD_DIM)
        for h in range(hk * group, (hk + 1) * group):
            s = _dot_nt(q[:, q_cols[h] * HEAD_DIM:(q_cols[h] + 1) * HEAD_DIM], k[:, ks])
            sink = jnp.broadcast_to(sink_ref[0:1, h:h + 1], (l, 1)) if use_sink else None
            outs.append(_softmax_pv([(s, v[:, ks])], sink))
    o_ref[...] = jnp.concatenate(outs, axis=1).astype(o_ref.dtype)


def _ctx_attention(q, k, v, sink, layer, use_sink, q_cols):
    b, l, qw = q.shape
    kw = k.shape[2]
    n_kv = kw // HEAD_DIM
    group = (qw // HEAD_DIM) // n_kv
    return pl.pallas_call(
        functools.partial(_ctx_attn_kernel, n_kv=n_kv, group=group, use_sink=use_sink, q_cols=q_cols),
        grid=(b,),
        in_specs=[pl.BlockSpec((None, l, qw), lambda bi: (bi, 0, 0)),
                  pl.BlockSpec((None, l, kw), lambda bi: (bi, 0, 0)),
                  pl.BlockSpec((None, l, kw), lambda bi: (bi, 0, 0)),
                  _slab(sink.shape[1:], layer)],
        out_specs=pl.BlockSpec((None, l, qw), lambda bi: (bi, 0, 0)),
        out_shape=jax.ShapeDtypeStruct((b, l, qw), BF16),
        compiler_params=_params(1),
        name="ctx_attention",
    )(q, k, v, sink)


def _merge_kernel(x_ref, mod_ref, g_ref, pp_ref, pc_ref, pn_ref, ab_ref, caw_ref,
                  yb_ref, hf_ref, hb_ref, cg_ref, yd_ref,
                  bg_ref, wbr_ref, wo_ref, *rest, sub):
    wgl_refs, o_ref = rest[:-1], rest[-1]
    i = pl.program_id(1)
    nt = pl.num_programs(1)
    t, d = x_ref.shape
    prev = jnp.where(i > 0, pp_ref[...], 0.0)
    nxt = jnp.where(i < nt - 1, pn_ref[...], 0.0)
    ext = jnp.concatenate([prev, pc_ref[...], nxt], axis=0)
    for r0 in range(0, t, sub):
        rows = slice(r0, r0 + sub)
        x = x_ref[rows, :]
        h = _adaln(x, g_ref[...], mod_ref[3:4, :], mod_ref[4:5, :]).astype(BF16)
        conv = (ext[HALO - 1 + r0:HALO - 1 + r0 + sub] * caw_ref[0:1, :]
                + ext[HALO + r0:HALO + r0 + sub] * caw_ref[1:2, :]
                + ext[HALO + 1 + r0:HALO + 1 + r0 + sub] * caw_ref[2:3, :])
        y_a = (ab_ref[rows, :] * conv).astype(BF16)
        y_c = ((hf_ref[rows, :] + hb_ref[rows, :]) * _gelu_tanh(cg_ref[rows, :])).astype(BF16)
        ys = ((y_a, _dot), (yb_ref[:, rows], _dot_tn), (y_c, _dot), (yd_ref[:, rows], _dot_tn))
        out = None
        for c0 in range(0, d, MERGE_CHUNK):
            merged = None
            for n, (y, dot) in enumerate(ys):
                gc = slice(n * d + c0, n * d + c0 + MERGE_CHUNK)
                gate = _sigmoid(_dot(h, wgl_refs[(n * d + c0) // MERGE_CHUNK][...]) + bg_ref[0:1, gc])
                term = gate * dot(y, wbr_ref[n, :, c0:c0 + MERGE_CHUNK])
                merged = term if merged is None else merged + term
            part = _dot(merged.astype(BF16), wo_ref[c0:c0 + MERGE_CHUNK, :])
            out = part if out is None else out + part
        o_ref[rows, :] = x + mod_ref[5:6, :] * out


def _merge(x, mods, norm_g, pa, ab, caw, yb, hf, hb, cg, yd, w_in, bg, wbr, wo, layer, ctx_row, tile, sub):
    b, s, d = x.shape
    mw = pa.shape[2]
    n_gate = N_BRANCH * d
    first, rem = divmod(w_in.shape[2] - n_gate, MERGE_CHUNK)
    assert rem == 0 and d % MERGE_CHUNK == 0
    gate_specs = [pl.BlockSpec((None, d, MERGE_CHUNK), lambda *_, k=k: (layer, 0, first + k),
                               pipeline_mode=pl.Buffered(1)) for k in range(n_gate // MERGE_CHUNK)]
    tok = lambda width: pl.BlockSpec((None, tile, width), lambda bi, i: (bi, i, 0))
    tok_t = pl.BlockSpec((None, mw, tile), lambda bi, i: (bi, 0, i))
    return pl.pallas_call(
        functools.partial(_merge_kernel, sub=min(sub, tile)),
        grid=(b, s // tile),
        in_specs=([tok(d), _mod_spec(d, layer, ctx_row), _slab((1, d), layer, 1)]
                  + _halo_specs(tile, mw, s, lambda i: i)
                  + [tok(mw), _slab((CONV_K, mw), layer),
                     tok_t, tok(mw), tok(mw), tok(mw), tok_t,
                     _slab(bg.shape[1:], layer), _slab(wbr.shape[1:], layer), _slab(wo.shape[1:], layer)]
                  + gate_specs),
        out_specs=tok(d),
        out_shape=jax.ShapeDtypeStruct((b, s, d), F32),
        compiler_params=_params(2),
        name="merge",
    )(x, mods, norm_g, pa, pa, pa, ab, caw, yb, hf, hb, cg, yd, bg, wbr, wo, *([w_in] * len(gate_specs)))


def _rope_tables(n_tok, width):
    half = HEAD_DIM // 2
    nf = half // 2
    inv_freq = ROPE_BASE ** (-np.arange(nf, dtype=np.float64) / nf)
    pos = np.arange(n_tok)
    ang_r = (pos // GRID_W).astype(np.float64)[:, None] * inv_freq[None, :]
    ang_c = (pos % GRID_W).astype(np.float64)[:, None] * inv_freq[None, :]
    cos = np.concatenate([np.cos(ang_r)] * 2 + [np.cos(ang_c)] * 2, axis=1)
    sin = np.concatenate([-np.sin(ang_r), np.sin(ang_r), -np.sin(ang_c), np.sin(ang_c)], axis=1)
    reps = width // HEAD_DIM
    return (jnp.asarray(np.tile(cos, (1, reps)).astype(np.float32)),
            jnp.asarray(np.tile(sin, (1, reps)).astype(np.float32)))


def _block_diag(w):
    *lead, n, c, e = w.shape
    eye = jnp.eye(n, dtype=w.dtype)
    return (eye[:, None, :, None] * w[..., :, :, None, :]).reshape(*lead, n * c, n * e)


def _pick_tile(n, target):
    t = min(n, target)
    while n % t:
        t //= 2
    return t


def kernel(x, c, ctx, c_ctx, w_mod, b_mod, norm_g, ffn_w_gate, ffn_w_up, ffn_w_down, w_in, b_gate,
           conv_a_w, qk_norm_g, attn_sink, lru_conv_w, lru_conv_b, lru_w_a, lru_b_a, lru_w_x, lru_b_x,
           lru_lam, na_rel_bias, w_branch, w_out):
    b, s, d = x.shape
    lx = ctx.shape[1]
    depth = w_mod.shape[0]
    mw = d // 4
    kvw = mw // 2
    rows = s // GRID_W
    assert s % GRID_W == 0 and rows >= 4 * NA_ROWS and lx % HALO == 0

    tile = _pick_tile(s, 512)
    tile_big = _pick_tile(s, 1024)
    tile_x = _pick_tile(lx, 512)
    tile_cf = _pick_tile(b * lx, 1024)
    tile_win = _pick_tile(s, 1024)
    nbr_rows = 4
    tile_nbr = nbr_rows * GRID_W
    nbr_sub = max(1, _pick_tile(s, 1024) // tile_nbr)

    assert b + 1 <= 8
    cs_t = jnp.zeros((d, 8), F32).at[:, :b].set(c.T).at[:, b].set(c_ctx)
    mods = _modulation(cs_t, b + 1, w_mod, b_mod).reshape(depth, 8, N_MOD, d)
    lat, cx_row = None, b

    cos, sin = _rope_tables(s, kvw)
    cos_x, sin_x = jnp.ones((lx, kvw), F32), jnp.zeros((lx, kvw), F32)
    lane = np.arange(mw)
    mhead = jnp.asarray((lane[:, None] // HEAD_DIM == lane[None, :] // HEAD_DIM) / HEAD_DIM, dtype=BF16)

    ng = norm_g.reshape(depth, 3, 1, d)
    wg, wu, wd = ffn_w_gate.astype(BF16), ffn_w_up.astype(BF16), ffn_w_down.astype(BF16)
    w_in_b = w_in.astype(BF16)
    qkg = jnp.tile(qk_norm_g, (1, 1, mw // HEAD_DIM))
    sink = (attn_sink * LOG2E).reshape(depth, 1, -1)
    sink_rows = jnp.repeat(sink.reshape(depth, 2, 1, 2), WINDOW, axis=3)
    lru_par = (lru_conv_w, lru_conv_b.reshape(depth, 1, mw), _block_diag(lru_w_a).astype(BF16), lru_b_a,
               _block_diag(lru_w_x).astype(BF16), lru_b_x, lru_lam)
    slabs = _nbr_slabs(na_rel_bias * LOG2E)
    slab_index = _nbr_slab_index(rows, nbr_rows)
    bg = b_gate.reshape(depth, 1, -1)
    wbr = w_branch.astype(BF16)
    wo = w_out.astype(BF16)

    xc = ctx
    for l in range(depth):
        ctx_out = l < depth - 1
        x = _ffn(x, mods, ng, wg, wu, wd, l, 0, lat, tile_big, tile)
        xc = _ffn(xc.reshape(1, b * lx, d), mods, ng, wg, wu, wd, l, 0, cx_row,
                  tile_cf, tile).reshape(b, lx, d)
        (pa, ab, qr, qp, kr, vb, cxl, cg, dq, dk, dv) = _inproj(
            x, mods, ng, w_in_b, qkg, mhead, cos, sin, l, lat, tile_big, tile)
        (pa_c, ab_c, _, qp_c, k_c, vb_c, cx_c, cg_c, dq_c, dk_c, dv_c) = _inproj(
            xc, mods, ng, w_in_b, qkg, mhead, cos_x, sin_x, l, cx_row, tile_x, tile_x)

        hf_c, hb_c = _scan(cx_c, jnp.zeros((b, 2, mw), F32), *lru_par, l, tile_x)
        h0 = jnp.stack([hf_c[:, lx - 1], hb_c[:, 0]], axis=1)
        hf, hb = _scan(cxl, h0, *lru_par, l, tile_big)
        yb = _window_attention(qr, qp, kr, vb, k_c, vb_c, sink_rows, l, tile_win)
        yd = _nbr_attention(dq, dk, dv, dk_c, dv_c, slabs, slab_index, l, tile_nbr, nbr_sub)
        x = _merge(x, mods, ng, pa, ab, conv_a_w, yb, hf, hb, cg, yd, w_in_b, bg, wbr, wo, l, lat, tile_big, tile)
        x = _ffn(x, mods, ng, wg, wu, wd, l, 1, lat, tile_big, tile)
        if ctx_out:
            yb_c = jnp.swapaxes(_ctx_attention(qp_c, k_c, vb_c, sink, l, True, WIN_Q_COLS), 1, 2)
            yd_c = jnp.swapaxes(_ctx_attention(dq_c, dk_c, dv_c, sink, l, False, (0, 1, 2, 3)), 1, 2)
            xc = _merge(xc, mods, ng, pa_c, ab_c, conv_a_w, yb_c, hf_c, hb_c, cg_c, yd_c,
                        w_in_b, bg, wbr, wo, l, cx_row, tile_x, tile_x)
            xc = _ffn(xc.reshape(1, b * lx, d), mods, ng, wg, wu, wd, l, 1, cx_row,
                      tile_cf, tile).reshape(b, lx, d)
    return x
```

```python
import functools

import jax
import jax.numpy as jnp
import numpy as np
from jax import lax
from jax.experimental import pallas as pl
from jax.experimental.pallas import tpu as pltpu

HEAD_DIM = 64
GRID_W = 64
WINDOW = 128
NA_ROWS = 8
NA_COLS = 16
N_BRANCH = 4
CONV_K = 3
LRU_CONV_K = 4
LRU_C = 8.0
ROPE_BASE = 10000.0
EPS = 1e-6
NEG_INF = -1e30
N_MOD = 9
LOG2E = 1.4426950408889634

V7X_VMEM_LIMIT_BYTES = 60 * 1024 * 1024
HALO = 8
CONV_DELAY = HALO
LANES = 128
MERGE_CHUNK = 512

F32 = jnp.float32
BF16 = jnp.bfloat16


def _params(n_grid):
    return pltpu.CompilerParams(
        dimension_semantics=("arbitrary",) * n_grid,
        vmem_limit_bytes=V7X_VMEM_LIMIT_BYTES)


def _dot(a, b):
    return jnp.dot(a, b, preferred_element_type=F32)


def _dot_nt(a, b):
    return lax.dot_general(a, b, (((1,), (1,)), ((), ())), preferred_element_type=F32)


def _sigmoid(x):
    return jax.nn.sigmoid(x)


def _gelu_tanh(x):
    return 0.5 * x * (1.0 + jnp.tanh(0.7978845608028654 * (x + 0.044715 * (x * x * x))))


def _adaln(x, g, shift, scale):
    ms = jnp.mean(x * x, axis=-1, keepdims=True)
    return (x * lax.rsqrt(ms + EPS) * g) * (1.0 + scale) + shift


def _full(shape):
    n = len(shape)
    return pl.BlockSpec(shape, lambda *_: (0,) * n)


def _slab(shape, *lead):
    n = len(shape)
    return pl.BlockSpec((None,) * len(lead) + tuple(shape), lambda *_: tuple(lead) + (0,) * n,
                        pipeline_mode=pl.Buffered(1))


def _mod_spec(d, layer, ctx_row):
    if ctx_row is None:
        return pl.BlockSpec((None, None, N_MOD, d), lambda bi, i: (layer, bi, 0, 0))
    return pl.BlockSpec((None, None, N_MOD, d), lambda bi, i: (layer, ctx_row, 0, 0))


def _mod_kernel(ct_ref, w_ref, b_ref, o_ref, *, n_rows):
    ct = ct_ref[...]
    st = ct * _sigmoid(ct)
    w = w_ref[...]
    rows = [jnp.sum(w * st[:, r:r + 1], axis=0, keepdims=True) for r in range(n_rows)]
    rows += [jnp.zeros_like(rows[0])] * (o_ref.shape[0] - n_rows)
    o_ref[...] = jnp.concatenate(rows, axis=0) + b_ref[...]


def _modulation(cs_t, n_rows, w_mod, b_mod):
    depth, d, n = w_mod.shape
    tn = n // 8
    return pl.pallas_call(
        functools.partial(_mod_kernel, n_rows=n_rows),
        grid=(depth, n // tn),
        in_specs=[pl.BlockSpec((d, 8), lambda l, j: (0, 0)),
                  pl.BlockSpec((None, d, tn), lambda l, j: (l, 0, j)),
                  pl.BlockSpec((None, 1, tn), lambda l, j: (l, 0, j))],
        out_specs=pl.BlockSpec((None, 8, tn), lambda l, j: (l, 0, j)),
        out_shape=jax.ShapeDtypeStruct((depth, 8, n), F32),
        compiler_params=_params(2),
        name="modulation",
    )(cs_t, w_mod, b_mod.reshape(depth, 1, n))


def _ffn_kernel(x_ref, mod_ref, g_ref, wg_ref, wu_ref, wd_ref, o_ref, *, row0, chunks, sub):
    for r0 in range(0, x_ref.shape[0], sub):
        rows = slice(r0, r0 + sub)
        x = x_ref[rows, :]
        h = _adaln(x, g_ref[...], mod_ref[row0:row0 + 1, :], mod_ref[row0 + 1:row0 + 2, :]).astype(BF16)
        acc = None
        for (lo, hi) in chunks:
            a = _dot(h, wg_ref[:, lo:hi])
            u = _dot(h, wu_ref[:, lo:hi])
            act = (a * _sigmoid(a) * u).astype(BF16)
            part = _dot(act, wd_ref[lo:hi, :])
            acc = part if acc is None else acc + part
        o_ref[rows, :] = x + (0.5 * mod_ref[row0 + 2:row0 + 3, :]) * acc


def _ffn(x, mods, norm_g, wg, wu, wd, layer, which, ctx_row, tile, sub):
    b, s, d = x.shape
    f = wg.shape[3]
    half = (f // 2 + 255) // 256 * 256
    chunks = ((0, half), (half, f)) if half < f else ((0, f),)
    return pl.pallas_call(
        functools.partial(_ffn_kernel, row0=6 * which, chunks=chunks, sub=min(sub, tile)),
        grid=(b, s // tile),
        in_specs=[pl.BlockSpec((None, tile, d), lambda bi, i: (bi, i, 0)),
                  _mod_spec(d, layer, ctx_row),
                  _slab((1, d), layer, 2 * which),
                  _slab((d, f), layer, which), _slab((d, f), layer, which), _slab((f, d), layer, which)],
        out_specs=pl.BlockSpec((None, tile, d), lambda bi, i: (bi, i, 0)),
        out_shape=jax.ShapeDtypeStruct((b, s, d), F32),
        compiler_params=_params(2),
        name="ffn",
    )(x, mods, norm_g, wg, wu, wd)


def _head_rms(t, m_ref, g):
    n = t.shape[1]
    sq = t * t
    hi = sq.astype(BF16)
    lo = (sq - hi.astype(F32)).astype(BF16)
    m = m_ref[0:n, 0:n]
    ms = _dot(hi, m) + _dot(lo, m)
    return t * lax.rsqrt(ms + EPS) * g


def _rope(t, cos, sin_signed):
    n = t.shape[1]
    lane = lax.broadcasted_iota(jnp.int32, t.shape, 1)
    partner = jnp.where((lane % 32) < 16, pltpu.roll(t, n - 16, 1), pltpu.roll(t, 16, 1))
    return t * cos + partner * sin_signed


def _inproj_kernel(x_ref, mod_ref, g_ref, w_ref, qkg_ref, m_ref, cos_ref, sin_ref, cw_ref, cb_ref,
                   pa_ref, ab_ref, qr_ref, qp_ref, kr_ref, vb_ref, us_ref, ut_ref, cg_ref,
                   dq_ref, dk_ref, dv_ref, tail_ref, *, mw, sub):
    i = pl.program_id(1)
    nt = pl.num_programs(1)
    t = x_ref.shape[0]
    kvw = mw // 2
    scale = LOG2E * HEAD_DIM ** -0.5

    @pl.when(i == 0)
    def _():
        tail_ref[...] = jnp.zeros_like(tail_ref)

    cx_groups = []
    for r0 in range(0, t, sub):
        rows = slice(r0, r0 + sub)
        h = _adaln(x_ref[rows, :], g_ref[...], mod_ref[3:4, :], mod_ref[4:5, :]).astype(BF16)
        u = _dot(h, w_ref[...])
        o = 0
        ax = u[:, o:o + mw]; o += mw
        ab = u[:, o:o + mw]; o += mw
        ac = u[:, o:o + mw]; o += mw
        bq = u[:, o:o + mw]; o += mw
        qa, qb = bq[:, :LANES], bq[:, LANES:]
        low = _half_mask(qa.shape, True)
        bq = jnp.concatenate([jnp.where(low, qa, pltpu.roll(qb, HEAD_DIM, 1)),
                              jnp.where(low, pltpu.roll(qa, HEAD_DIM, 1), qb)], axis=1)
        bk = u[:, o:o + kvw]; o += kvw
        bv = u[:, o:o + kvw]; o += kvw
        cx = u[:, o:o + mw]; o += mw
        cg = u[:, o:o + mw]; o += mw
        dq = u[:, o:o + mw]; o += mw
        dk = u[:, o:o + mw]; o += mw
        dv = u[:, o:o + mw]; o += mw
        pa_ref[rows, :] = ac * ax
        ab_ref[rows, :] = ab
        cx_groups.append(cx)
        cg_ref[rows, :] = cg
        cos = cos_ref[rows, :]
        sin = sin_ref[rows, :]
        cos2 = jnp.concatenate([cos, cos], axis=1)
        sin2 = jnp.concatenate([sin, sin], axis=1)
        q = _head_rms(bq, m_ref, qkg_ref[0:1, :]) * scale
        qp_ref[rows, :] = q.astype(BF16)
        qr_ref[rows, :] = _rope(q, cos2, sin2).astype(BF16)
        k = _head_rms(bk, m_ref, qkg_ref[1:2, 0:kvw])
        kr_ref[rows, :] = _rope(k, cos, sin).astype(BF16)
        vb_ref[rows, :] = bv.astype(BF16)
        dq_ref[rows, :] = (_head_rms(dq, m_ref, qkg_ref[2:3, :]) * scale).astype(BF16)
        dk_ref[rows, :] = _head_rms(dk, m_ref, qkg_ref[3:4, :]).astype(BF16)
        dv_ref[rows, :] = dv.astype(BF16)

    def conv(ext, n):
        lead = 2 * CONV_DELAY - CONV_DELAY - LRU_CONV_K // 2
        acc = cb_ref[...]
        for k in range(LRU_CONV_K):
            acc = acc + ext[lead + k:lead + k + n] * cw_ref[k:k + 1, :]
        return acc

    cx_tile = jnp.concatenate(cx_groups, axis=0)
    us_ref[...] = conv(jnp.concatenate([tail_ref[...], cx_tile], axis=0), t)
    tail = cx_tile[t - 2 * CONV_DELAY:, :]
    tail_ref[...] = tail

    @pl.when(i == nt - 1)
    def _():
        ut_ref[...] = conv(jnp.concatenate([tail, jnp.zeros((CONV_DELAY, mw), F32)], axis=0), CONV_DELAY)


def _inproj(x, mods, norm_g, w_in, qkg, mhead, cos, sin, cw, cb, layer, ctx_row, tile, sub):
    b, s, d = x.shape
    mw = qkg.shape[2]
    kvw = mw // 2
    nw = 9 * mw + 2 * kvw
    assert mw == 2 * LANES and tile >= 2 * CONV_DELAY
    tok = lambda width: pl.BlockSpec((None, tile, width), lambda bi, i: (bi, i, 0))
    end = pl.BlockSpec((None, CONV_DELAY, mw), lambda bi, i: (bi, 0, 0))
    widths = (mw, mw, mw, mw, kvw, kvw, mw, None, mw, mw, mw, mw)
    dtypes = (F32, F32, BF16, BF16, BF16, BF16, F32, F32, F32, BF16, BF16, BF16)
    return pl.pallas_call(
        functools.partial(_inproj_kernel, mw=mw, sub=min(sub, tile)),
        grid=(b, s // tile),
        in_specs=[tok(d),
                  _mod_spec(d, layer, ctx_row),
                  _slab((1, d), layer, 1), _slab((d, nw), layer), _slab((4, mw), layer), _full((mw, mw)),
                  pl.BlockSpec((tile, kvw), lambda bi, i: (i, 0)),
                  pl.BlockSpec((tile, kvw), lambda bi, i: (i, 0)),
                  _slab((LRU_CONV_K, mw), layer), _slab((1, mw), layer)],
        out_specs=[end if wd is None else tok(wd) for wd in widths],
        out_shape=[jax.ShapeDtypeStruct((b, CONV_DELAY, mw) if wd is None else (b, s, wd), dt)
                   for wd, dt in zip(widths, dtypes)],
        scratch_shapes=[pltpu.VMEM((2 * CONV_DELAY, mw), F32)],
        compiler_params=_params(2),
        name="inproj",
    )(x, mods, norm_g, w_in, qkg, mhead, cos, sin, cw, cb)


def _shift_rows(t, d, fill, reverse):
    n = t.shape[0]
    row = lax.broadcasted_iota(jnp.int32, t.shape, 0)
    if reverse:
        return jnp.where(row < n - d, pltpu.roll(t, n - d, 0), fill)
    return jnp.where(row >= d, pltpu.roll(t, d, 0), fill)


def _lru_coeffs(cur_ref, next_ref, end_ref, is_last, wa_ref, ba_ref, wx_ref, bx_ref, lam_ref, direction):
    follow = jnp.where(is_last, end_ref[...], next_ref[...])
    u = jnp.concatenate([cur_ref[CONV_DELAY:, :], follow], axis=0)
    ub = u.astype(BF16)
    r = _sigmoid(_dot(ub, wa_ref[direction]) + ba_ref[direction:direction + 1, :])
    ig = _sigmoid(_dot(ub, wx_ref[direction]) + bx_ref[direction:direction + 1, :])
    nl = -lam_ref[direction:direction + 1, :]
    softplus = jnp.maximum(nl, 0.0) + jnp.log1p(jnp.exp(-jnp.abs(nl)))
    neg_log_a = r * (LRU_C * softplus)
    a = jnp.exp2(r * ((-LRU_C * LOG2E) * softplus))
    z = jnp.tanh(neg_log_a) * (a * a + 1.0)
    return a, jnp.where(z > 0.0, z * lax.rsqrt(z), 0.0) * (ig * u)


def _scan_tile(a, bb, carry, reverse, sa_ref, sb_ref, sc_ref, out_ref):
    t, w = a.shape
    g = t // HALO
    a3 = a.reshape(g, HALO, w)
    b3 = bb.reshape(g, HALO, w)
    sub = lax.broadcasted_iota(jnp.int32, (g, HALO, w), 1)
    d = 1
    while d < HALO:
        ok = (sub < HALO - d) if reverse else (sub >= d)
        shift = HALO - d if reverse else d
        a_r = pltpu.roll(a3, shift, 1)
        b_r = pltpu.roll(b3, shift, 1)
        b3 = jnp.where(ok, a3 * b_r + b3, b3)
        a3 = jnp.where(ok, a3 * a_r, a3)
        d *= 2
    sa_ref[...] = a3.reshape(t, w)
    sb_ref[...] = b3.reshape(t, w)
    edge = 0 if reverse else HALO - 1
    at = sa_ref[pl.ds(edge, g, stride=HALO), :]
    bt = sb_ref[pl.ds(edge, g, stride=HALO), :]
    d = 1
    while d < g:
        a_s = _shift_rows(at, d, 1.0, reverse)
        b_s = _shift_rows(bt, d, 0.0, reverse)
        bt = at * b_s + bt
        at = at * a_s
        d *= 2
    h_end = at * carry + bt
    sc_ref[...] = _shift_rows(h_end, 1, carry, reverse)
    for gi in range(g):
        rows = slice(gi * HALO, (gi + 1) * HALO)
        out_ref[rows, :] = sa_ref[rows, :] * sc_ref[gi:gi + 1, :] + sb_ref[rows, :]
    return h_end[0:1, :] if reverse else h_end[g - 1:g, :]


def _scan_kernel(fc_ref, fn_ref, rc_ref, rn_ref, end_ref, h0_ref,
                 wa_ref, ba_ref, wx_ref, bx_ref, lam_ref, hf_ref, hb_ref,
                 carry_ref, sa_ref, sb_ref, sc_ref):
    i = pl.program_id(1)
    nt = pl.num_programs(1)

    @pl.when(i == 0)
    def _():
        carry_ref[...] = h0_ref[...]

    par = (wa_ref, ba_ref, wx_ref, bx_ref, lam_ref)
    tiles = ((fc_ref, fn_ref, i == nt - 1, hf_ref), (rc_ref, rn_ref, i == 0, hb_ref))
    for direction, (c_ref, n_ref, is_last, out_ref) in enumerate(tiles):
        a, bb = _lru_coeffs(c_ref, n_ref, end_ref, is_last, *par, direction)
        for j in range(a.shape[1] // LANES):
            cols = slice(j * LANES, (j + 1) * LANES)
            carry_ref[direction:direction + 1, cols] = _scan_tile(
                a[:, cols], bb[:, cols], carry_ref[direction:direction + 1, cols], direction == 1,
                sa_ref.at[direction, j], sb_ref.at[direction, j], sc_ref.at[direction, j],
                out_ref.at[:, cols])


def _halo_specs(tile, width, n_rows, tile_index):
    r = tile // HALO
    last = n_rows // HALO - 1
    prev = pl.BlockSpec((None, HALO, width),
                        lambda bi, i: (bi, jnp.maximum(tile_index(i) * r - 1, 0), 0))
    cur = pl.BlockSpec((None, tile, width), lambda bi, i: (bi, tile_index(i), 0))
    nxt = pl.BlockSpec((None, HALO, width),
                       lambda bi, i: (bi, jnp.minimum((tile_index(i) + 1) * r, last), 0))
    return [prev, cur, nxt]


def _scan(us, ut, h0, wa, ba, wx, bx, lam, layer, tile):
    b, s, w = us.shape
    nt = s // tile
    fwd = lambda i: i
    rev = lambda i: nt - 1 - i
    return pl.pallas_call(
        _scan_kernel,
        grid=(b, nt),
        in_specs=(_halo_specs(tile, w, s, fwd)[1:] + _halo_specs(tile, w, s, rev)[1:]
                  + [pl.BlockSpec((None, CONV_DELAY, w), lambda bi, i: (bi, 0, 0)),
                     pl.BlockSpec((None, 2, w), lambda bi, i: (bi, 0, 0)),
                     _slab((2, w, w), layer), _slab((2, w), layer),
                     _slab((2, w, w), layer), _slab((2, w), layer), _slab((2, w), layer)]),
        out_specs=[pl.BlockSpec((None, tile, w), lambda bi, i: (bi, i, 0)),
                   pl.BlockSpec((None, tile, w), lambda bi, i: (bi, nt - 1 - i, 0))],
        out_shape=[jax.ShapeDtypeStruct((b, s, w), F32)] * 2,
        scratch_shapes=[pltpu.VMEM((2, w), F32),
                        pltpu.VMEM((2, w // LANES, tile, LANES), F32),
                        pltpu.VMEM((2, w // LANES, tile, LANES), F32),
                        pltpu.VMEM((2, w // LANES, tile // HALO, LANES), F32)],
        compiler_params=_params(2),
        name="lru_scan",
    )(us, us, us, us, ut, h0, wa, ba, wx, bx, lam)


def _softmax_pv(parts, sink):
    m = None
    for s, _ in parts:
        mx = jnp.max(s, axis=-1, keepdims=True)
        m = mx if m is None else jnp.maximum(m, mx)
    if sink is not None:
        m = jnp.maximum(m, sink)
    den = None
    out = None
    for s, v in parts:
        p = jnp.exp2(s - m)
        sm = jnp.sum(p, axis=-1, keepdims=True)
        den = sm if den is None else den + sm
        pv = _dot(p.astype(BF16), v)
        out = pv if out is None else out + pv
    if sink is not None:
        den = den + jnp.exp2(sink - m)
    return out / den


def _dot_tn(a, b):
    return lax.dot_general(a, b, (((0,), (0,)), ((), ())), preferred_element_type=F32)


def _half_mask(shape, lower):
    lane = lax.broadcasted_iota(jnp.int32, shape, 1) % LANES
    return (lane < HEAD_DIM) if lower else (lane >= HEAD_DIM)


def _softmax_pv_t(parts, sink_row, lower):
    m = None
    for s, _ in parts:
        mx = jnp.max(s, axis=0, keepdims=True)
        m = mx if m is None else jnp.maximum(m, mx)
    if sink_row is not None:
        m = jnp.maximum(m, sink_row)
    res = None
    for s, v in parts:
        r = _dot_tn(v, jnp.exp2(s - m).astype(BF16))
        res = r if res is None else res + r
    out, den = (res[:HEAD_DIM], res[HEAD_DIM:HEAD_DIM + 1]) if lower else (res[HEAD_DIM:], res[0:1])
    if sink_row is not None:
        den = den + jnp.exp2(sink_row - m)
    return out * (1.0 / den)


WIN_Q_COLS = (0, 2, 1, 3)


def _window_kernel(qr_ref, qp_ref, kp_ref, kc_ref, kn_ref, vp_ref, vc_ref, vn_ref,
                   kx_ref, vx_ref, sink_ref, band_ref, o_ref):
    i = pl.program_id(1)
    nt = pl.num_programs(1)
    tq = qr_ref.shape[0]
    nblk = tq // WINDOW
    qr = qr_ref[...]
    qp = qp_ref[...]
    kcat = jnp.concatenate([kp_ref[...], kc_ref[...], kn_ref[...]], axis=0)
    vcat = jnp.concatenate([vp_ref[...], vc_ref[...], vn_ref[...]], axis=0)
    kx = kx_ref[...]
    vx = vx_ref[...]
    for hk in range(2):
        lower = hk == 0
        qm = _half_mask((tq, LANES), lower)
        zero = jnp.zeros((tq, LANES), BF16)
        qra, qrb = jnp.where(qm, qr[:, :LANES], zero), jnp.where(qm, qr[:, LANES:], zero)
        qpa, qpb = jnp.where(qm, qp[:, :LANES], zero), jnp.where(qm, qp[:, LANES:], zero)
        vw = jnp.where(_half_mask(vcat.shape, lower), vcat, jnp.ones_like(vcat))
        vxw = jnp.where(_half_mask(vx.shape, lower), vx, jnp.ones_like(vx))
        s_ctx = _dot_nt(kx, jnp.concatenate([qpa, qpb], axis=0))
        for j in range(nblk):
            rq = slice(j * WINDOW, (j + 1) * WINDOW)
            rk = slice(j * WINDOW, (j + 3) * WINDOW)
            if j == 0:
                band = band_ref[jnp.where(i == 0, 1, 0)]
            elif j == nblk - 1:
                band = band_ref[jnp.where(i == nt - 1, 2, 0)]
            else:
                band = band_ref[0]
            s_loc = (_dot_nt(kcat[rk], jnp.concatenate([qra[rq], qrb[rq]], axis=0))
                     + jnp.concatenate([band, band], axis=1))
            s_cx = jnp.concatenate([s_ctx[:, rq], s_ctx[:, tq + j * WINDOW:tq + (j + 1) * WINDOW]], axis=1)
            o = _softmax_pv_t([(s_loc, vw[rk]), (s_cx, vxw)], sink_ref[hk], lower).astype(o_ref.dtype)
            ha, hb = 2 * hk, 2 * hk + 1
            o_ref[ha * HEAD_DIM:(ha + 1) * HEAD_DIM, rq] = o[:, :WINDOW]
            o_ref[hb * HEAD_DIM:(hb + 1) * HEAD_DIM, rq] = o[:, WINDOW:]


def _window_band():
    kk = np.arange(3 * WINDOW)[:, None]
    r = np.arange(WINDOW)[None, :]
    ok = np.abs(kk - WINDOW - r) <= WINDOW
    variants = [ok, ok & (kk >= WINDOW), ok & (kk < 2 * WINDOW)]
    return jnp.asarray(np.where(np.stack(variants), 0.0, NEG_INF).astype(np.float32))


def _window_attention(qr, qp, k, v, kx, vx, sink_rows, layer, tile):
    b, s, qw = qr.shape
    kw = k.shape[2]
    lx = kx.shape[1]
    assert kw == LANES and qw == 2 * LANES and tile % WINDOW == 0 and s >= 2 * WINDOW
    r = tile // WINDOW
    last = s // WINDOW - 1
    band = _window_band()
    qspec = pl.BlockSpec((None, tile, qw), lambda bi, i: (bi, i, 0))
    prev = pl.BlockSpec((None, WINDOW, kw), lambda bi, i: (bi, jnp.maximum(i * r - 1, 0), 0))
    cur = pl.BlockSpec((None, tile, kw), lambda bi, i: (bi, i, 0))
    nxt = pl.BlockSpec((None, WINDOW, kw), lambda bi, i: (bi, jnp.minimum((i + 1) * r, last), 0))
    ctx = pl.BlockSpec((None, lx, kw), lambda bi, i: (bi, 0, 0))
    return pl.pallas_call(
        _window_kernel,
        grid=(b, s // tile),
        in_specs=[qspec, qspec, prev, cur, nxt, prev, cur, nxt, ctx, ctx,
                  _slab(sink_rows.shape[1:], layer), _full(band.shape)],
        out_specs=pl.BlockSpec((None, qw, tile), lambda bi, i: (bi, 0, i)),
        out_shape=jax.ShapeDtypeStruct((b, qw, s), BF16),
        compiler_params=_params(2),
        name="window_attention",
    )(qr, qp, k, k, k, v, v, v, kx, vx, sink_rows, band)


NBR_MASKED = 2 * NA_ROWS - 1


def _nbr_kernel(q_ref, kp_ref, kc_ref, kn_ref, vp_ref, vc_ref, vn_ref, kx_ref, vx_ref, sl_ref, sr_ref,
                o_ref, bias_ref, *, slab_index):
    i = pl.program_id(1)
    nt = pl.num_programs(1)
    tq = kp_ref.shape[0]
    n_sub = q_ref.shape[0] // tq
    tile_rows = tq // GRID_W

    @pl.when((pl.program_id(0) == 0) & (i == 0))
    def _():
        for v in range(3):
            for g in range(q_ref.shape[1] // LANES):
                for bj in range(3 * tile_rows):
                    for e in range(2):
                        for pp in range(tile_rows // 2):
                            left = slab_index[v][2 * pp][bj // tile_rows][bj % tile_rows]
                            right = slab_index[v][2 * pp + 1][bj // tile_rows][bj % tile_rows]
                            bias_ref[v, g, bj * GRID_W:(bj + 1) * GRID_W,
                                     e * tq + pp * LANES:e * tq + (pp + 1) * LANES] = (
                                sl_ref[2 * g + e, left] + sr_ref[2 * g + e, right])

    kcat = jnp.concatenate([kp_ref[...], kc_ref[...], kn_ref[...]], axis=0)
    vcat = jnp.concatenate([vp_ref[...], vc_ref[...], vn_ref[...]], axis=0)
    kx = kx_ref[...]
    vx = vx_ref[...]
    for sub in range(n_sub):
        variant = 1
        if sub == 0:
            variant = jnp.where(i == 0, 0, variant)
        if sub == n_sub - 1:
            variant = jnp.where(i == nt - 1, 2, variant)
        q = q_ref[sub * tq:(sub + 1) * tq, :]
        rk = slice(sub * tq, (sub + 3) * tq)
        for g in range(q.shape[1] // LANES):
            cols = slice(g * LANES, (g + 1) * LANES)
            qg, vg, vxg = q[:, cols], vcat[rk, cols], vx[:, cols]
            low = _half_mask(qg.shape, True)
            zero = jnp.zeros_like(qg)
            qcat = jnp.concatenate([jnp.where(low, qg, zero), jnp.where(low, zero, qg)], axis=0)
            s_loc = _dot_nt(kcat[rk, cols], qcat) + bias_ref[variant, g]
            s_ctx = _dot_nt(kx[:, cols], qcat)
            for e in range(2):
                lower = e == 0
                qs = slice(e * tq, (e + 1) * tq)
                vw = jnp.where(_half_mask(vg.shape, lower), vg, jnp.ones_like(vg))
                vxw = jnp.where(_half_mask(vxg.shape, lower), vxg, jnp.ones_like(vxg))
                o = _softmax_pv_t([(s_loc[:, qs], vw), (s_ctx[:, qs], vxw)], None, lower)
                h = 2 * g + e
                o_ref[h * HEAD_DIM:(h + 1) * HEAD_DIM, sub * tq:(sub + 1) * tq] = o.astype(o_ref.dtype)


def _nbr_slab_index(rows, tile_rows):
    n_tiles = rows // tile_rows
    index = []
    for ti in (0, 1, n_tiles - 1):
        r = ti * tile_rows + np.arange(tile_rows)
        start = np.clip(r - NA_ROWS // 2, 0, rows - NA_ROWS)
        blk = np.arange(3)[:, None] - 1 + ti
        krow = blk * tile_rows + np.arange(tile_rows)[None, :]
        ok = ((krow[None] >= start[:, None, None]) & (krow[None] < start[:, None, None] + NA_ROWS)
              & ((blk >= 0) & (blk < n_tiles))[None])
        dr = krow[None] - r[:, None, None] + NA_ROWS - 1
        assert ((dr >= 0) & (dr < NBR_MASKED))[ok].all()
        index.append(np.where(ok, dr, NBR_MASKED))
    return np.stack(index).tolist()


def _nbr_slabs(rel_bias):
    qc = np.arange(GRID_W)[None, :]
    kc = np.arange(GRID_W)[:, None]
    dc = np.clip(kc - qc, 1 - NA_COLS, NA_COLS - 1) + NA_COLS - 1
    c_start = np.clip(qc - NA_COLS // 2, 0, GRID_W - NA_COLS)
    col_ok = (kc >= c_start) & (kc < c_start + NA_COLS)
    one_hot = jnp.asarray(np.eye(2 * NA_COLS - 1, dtype=np.float32)[dc])
    slabs = jnp.einsum('kqc,lhrc->lhrkq', one_hot, rel_bias.astype(F32), precision=lax.Precision.HIGHEST)
    slabs = jnp.where(jnp.asarray(col_ok), slabs, NEG_INF)
    slabs = jnp.concatenate([slabs, jnp.full_like(slabs[:, :, :1], NEG_INF)], axis=2)
    zeros = jnp.zeros_like(slabs)
    return jnp.concatenate([slabs, zeros], axis=-1), jnp.concatenate([zeros, slabs], axis=-1)


def _nbr_attention(q, k, v, kx, vx, slabs, slab_index, layer, tile, n_sub):
    b, s, w = q.shape
    lx = kx.shape[1]
    step = n_sub * tile
    nt = s // step
    last = s // tile - 1
    left, right = slabs
    tok = pl.BlockSpec((None, step, w), lambda bi, i: (bi, i, 0))
    prev = pl.BlockSpec((None, tile, w), lambda bi, i: (bi, jnp.maximum(i * n_sub - 1, 0), 0))
    nxt = pl.BlockSpec((None, tile, w), lambda bi, i: (bi, jnp.minimum((i + 1) * n_sub, last), 0))
    ctx = pl.BlockSpec((None, lx, w), lambda bi, i: (bi, 0, 0))
    return pl.pallas_call(
        functools.partial(_nbr_kernel, slab_index=slab_index),
        grid=(b, nt),
        in_specs=[tok, prev, tok, nxt, prev, tok, nxt, ctx, ctx,
                  _slab(left.shape[1:], layer), _slab(right.shape[1:], layer)],
        out_specs=pl.BlockSpec((None, w, step), lambda bi, i: (bi, 0, i)),
        out_shape=jax.ShapeDtypeStruct((b, w, s), BF16),
        scratch_shapes=[pltpu.VMEM((3, w // LANES, 3 * tile, 2 * tile), F32)],
        compiler_params=_params(2),
        name="nbr_attention",
    )(q, k, k, k, v, v, v, kx, vx, left, right)


def _ctx_attn_kernel(q_ref, k_ref, v_ref, sink_ref, o_ref, *, n_kv, group, use_sink, q_cols):
    q = q_ref[...]
    k = k_ref[...]
    v = v_ref[...]
    l = q.shape[0]
    outs = []
    for hk in range(n_kv):
        ks = slice(hk * HEAD_DIM, (hk + 1) * HEAD_DIM)
        for h in range(hk * group, (hk + 1) * group):
            s = _dot_nt(q[:, q_cols[h] * HEAD_DIM:(q_cols[h] + 1) * HEAD_DIM], k[:, ks])
            sink = jnp.broadcast_to(sink_ref[0:1, h:h + 1], (l, 1)) if use_sink else None
            outs.append(_softmax_pv([(s, v[:, ks])], sink))
    o_ref[...] = jnp.concatenate(outs, axis=1).astype(o_ref.dtype)


def _ctx_attention(q, k, v, sink, layer, use_sink, q_cols):
    b, l, qw = q.shape
    kw = k.shape[2]
    n_kv = kw // HEAD_DIM
    group = (qw // HEAD_DIM) // n_kv
    return pl.pallas_call(
        functools.partial(_ctx_attn_kernel, n_kv=n_kv, group=group, use_sink=use_sink, q_cols=q_cols),
        grid=(b,),
        in_specs=[pl.BlockSpec((None, l, qw), lambda bi: (bi, 0, 0)),
                  pl.BlockSpec((None, l, kw), lambda bi: (bi, 0, 0)),
                  pl.BlockSpec((None, l, kw), lambda bi: (bi, 0, 0)),
                  _slab(sink.shape[1:], layer)],
        out_specs=pl.BlockSpec((None, l, qw), lambda bi: (bi, 0, 0)),
        out_shape=jax.ShapeDtypeStruct((b, l, qw), BF16),
        compiler_params=_params(1),
        name="ctx_attention",
    )(q, k, v, sink)


def _merge_kernel(x_ref, mod_ref, g_ref, pp_ref, pc_ref, pn_ref, ab_ref, caw_ref,
                  yb_ref, hf_ref, hb_ref, cg_ref, yd_ref,
                  bg_ref, wbr_ref, wo_ref, *rest, sub):
    wgl_refs, o_ref = rest[:-1], rest[-1]
    i = pl.program_id(1)
    nt = pl.num_programs(1)
    t, d = x_ref.shape
    prev = jnp.where(i > 0, pp_ref[...], 0.0)
    nxt = jnp.where(i < nt - 1, pn_ref[...], 0.0)
    ext = jnp.concatenate([prev, pc_ref[...], nxt], axis=0)
    for r0 in range(0, t, sub):
        rows = slice(r0, r0 + sub)
        x = x_ref[rows, :]
        h = _adaln(x, g_ref[...], mod_ref[3:4, :], mod_ref[4:5, :]).astype(BF16)
        conv = (ext[HALO - 1 + r0:HALO - 1 + r0 + sub] * caw_ref[0:1, :]
                + ext[HALO + r0:HALO + r0 + sub] * caw_ref[1:2, :]
                + ext[HALO + 1 + r0:HALO + 1 + r0 + sub] * caw_ref[2:3, :])
        y_a = (ab_ref[rows, :] * conv).astype(BF16)
        y_c = ((hf_ref[rows, :] + hb_ref[rows, :]) * _gelu_tanh(cg_ref[rows, :])).astype(BF16)
        ys = ((y_a, _dot), (yb_ref[:, rows], _dot_tn), (y_c, _dot), (yd_ref[:, rows], _dot_tn))
        out = None
        for c0 in range(0, d, MERGE_CHUNK):
            merged = None
            for n, (y, dot) in enumerate(ys):
                gc = slice(n * d + c0, n * d + c0 + MERGE_CHUNK)
                gate = _sigmoid(_dot(h, wgl_refs[(n * d + c0) // MERGE_CHUNK][...]) + bg_ref[0:1, gc])
                term = gate * dot(y, wbr_ref[n, :, c0:c0 + MERGE_CHUNK])
                merged = term if merged is None else merged + term
            part = _dot(merged.astype(BF16), wo_ref[c0:c0 + MERGE_CHUNK, :])
            out = part if out is None else out + part
        o_ref[rows, :] = x + mod_ref[5:6, :] * out


def _merge(x, mods, norm_g, pa, ab, caw, yb, hf, hb, cg, yd, w_in, bg, wbr, wo, layer, ctx_row, tile, sub):
    b, s, d = x.shape
    mw = pa.shape[2]
    n_gate = N_BRANCH * d
    first, rem = divmod(w_in.shape[2] - n_gate, MERGE_CHUNK)
    assert rem == 0 and d % MERGE_CHUNK == 0
    gate_specs = [pl.BlockSpec((None, d, MERGE_CHUNK), lambda *_, k=k: (layer, 0, first + k),
                               pipeline_mode=pl.Buffered(1)) for k in range(n_gate // MERGE_CHUNK)]
    tok = lambda width: pl.BlockSpec((None, tile, width), lambda bi, i: (bi, i, 0))
    tok_t = pl.BlockSpec((None, mw, tile), lambda bi, i: (bi, 0, i))
    return pl.pallas_call(
        functools.partial(_merge_kernel, sub=min(sub, tile)),
        grid=(b, s // tile),
        in_specs=([tok(d), _mod_spec(d, layer, ctx_row), _slab((1, d), layer, 1)]
                  + _halo_specs(tile, mw, s, lambda i: i)
                  + [tok(mw), _slab((CONV_K, mw), layer),
                     tok_t, tok(mw), tok(mw), tok(mw), tok_t,
                     _slab(bg.shape[1:], layer), _slab(wbr.shape[1:], layer), _slab(wo.shape[1:], layer)]
                  + gate_specs),
        out_specs=tok(d),
        out_shape=jax.ShapeDtypeStruct((b, s, d), F32),
        compiler_params=_params(2),
        name="merge",
    )(x, mods, norm_g, pa, pa, pa, ab, caw, yb, hf, hb, cg, yd, bg, wbr, wo, *([w_in] * len(gate_specs)))


def _rope_tables(n_tok, width):
    half = HEAD_DIM // 2
    nf = half // 2
    inv_freq = ROPE_BASE ** (-np.arange(nf, dtype=np.float64) / nf)
    pos = np.arange(n_tok)
    ang_r = (pos // GRID_W).astype(np.float64)[:, None] * inv_freq[None, :]
    ang_c = (pos % GRID_W).astype(np.float64)[:, None] * inv_freq[None, :]
    cos = np.concatenate([np.cos(ang_r)] * 2 + [np.cos(ang_c)] * 2, axis=1)
    sin = np.concatenate([-np.sin(ang_r), np.sin(ang_r), -np.sin(ang_c), np.sin(ang_c)], axis=1)
    reps = width // HEAD_DIM
    return (jnp.asarray(np.tile(cos, (1, reps)).astype(np.float32)),
            jnp.asarray(np.tile(sin, (1, reps)).astype(np.float32)))


def _block_diag(w):
    *lead, n, c, e = w.shape
    eye = jnp.eye(n, dtype=w.dtype)
    return (eye[:, None, :, None] * w[..., :, :, None, :]).reshape(*lead, n * c, n * e)


def _pick_tile(n, target):
    t = min(n, target)
    while n % t:
        t //= 2
    return t


def kernel(x, c, ctx, c_ctx, w_mod, b_mod, norm_g, ffn_w_gate, ffn_w_up, ffn_w_down, w_in, b_gate,
           conv_a_w, qk_norm_g, attn_sink, lru_conv_w, lru_conv_b, lru_w_a, lru_b_a, lru_w_x, lru_b_x,
           lru_lam, na_rel_bias, w_branch, w_out):
    b, s, d = x.shape
    lx = ctx.shape[1]
    depth = w_mod.shape[0]
    mw = d // 4
    kvw = mw // 2
    rows = s // GRID_W
    assert s % GRID_W == 0 and rows >= 4 * NA_ROWS and lx % HALO == 0

    tile = _pick_tile(s, 512)
    tile_big = _pick_tile(s, 1024)
    tile_x = _pick_tile(lx, 512)
    tile_cf = _pick_tile(b * lx, 1024)
    tile_win = _pick_tile(s, 1024)
    nbr_rows = 4
    tile_nbr = nbr_rows * GRID_W
    nbr_sub = max(1, _pick_tile(s, 1024) // tile_nbr)

    assert b + 1 <= 8
    cs_t = jnp.zeros((d, 8), F32).at[:, :b].set(c.T).at[:, b].set(c_ctx)
    mods = _modulation(cs_t, b + 1, w_mod, b_mod).reshape(depth, 8, N_MOD, d)
    lat, cx_row = None, b

    cos, sin = _rope_tables(s, kvw)
    cos_x, sin_x = jnp.ones((lx, kvw), F32), jnp.zeros((lx, kvw), F32)
    lane = np.arange(mw)
    mhead = jnp.asarray((lane[:, None] // HEAD_DIM == lane[None, :] // HEAD_DIM) / HEAD_DIM, dtype=BF16)

    ng = norm_g.reshape(depth, 3, 1, d)
    wg, wu, wd = ffn_w_gate.astype(BF16), ffn_w_up.astype(BF16), ffn_w_down.astype(BF16)
    w_in_b = w_in.astype(BF16)
    qkg = jnp.tile(qk_norm_g, (1, 1, mw // HEAD_DIM))
    sink = (attn_sink * LOG2E).reshape(depth, 1, -1)
    sink_rows = jnp.repeat(sink.reshape(depth, 2, 1, 2), WINDOW, axis=3)
    conv_par = (lru_conv_w, lru_conv_b.reshape(depth, 1, mw))
    lru_par = (_block_diag(lru_w_a).astype(BF16), lru_b_a, _block_diag(lru_w_x).astype(BF16), lru_b_x, lru_lam)
    slabs = _nbr_slabs(na_rel_bias * LOG2E)
    slab_index = _nbr_slab_index(rows, nbr_rows)
    bg = b_gate.reshape(depth, 1, -1)
    wbr = w_branch.astype(BF16)
    wo = w_out.astype(BF16)

    xc = ctx
    for l in range(depth):
        ctx_out = l < depth - 1
        x = _ffn(x, mods, ng, wg, wu, wd, l, 0, lat, tile_big, tile)
        xc = _ffn(xc.reshape(1, b * lx, d), mods, ng, wg, wu, wd, l, 0, cx_row,
                  tile_cf, tile).reshape(b, lx, d)
        (pa, ab, qr, qp, kr, vb, us, ut, cg, dq, dk, dv) = _inproj(
            x, mods, ng, w_in_b, qkg, mhead, cos, sin, *conv_par, l, lat, tile_big, tile)
        (pa_c, ab_c, _, qp_c, k_c, vb_c, us_c, ut_c, cg_c, dq_c, dk_c, dv_c) = _inproj(
            xc, mods, ng, w_in_b, qkg, mhead, cos_x, sin_x, *conv_par, l, cx_row, tile_x, tile_x)

        hf_c, hb_c = _scan(us_c, ut_c, jnp.zeros((b, 2, mw), F32), *lru_par, l, tile_x)
        h0 = jnp.stack([hf_c[:, lx - 1], hb_c[:, 0]], axis=1)
        hf, hb = _scan(us, ut, h0, *lru_par, l, tile_big)
        yb = _window_attention(qr, qp, kr, vb, k_c, vb_c, sink_rows, l, tile_win)
        yd = _nbr_attention(dq, dk, dv, dk_c, dv_c, slabs, slab_index, l, tile_nbr, nbr_sub)
        x = _merge(x, mods, ng, pa, ab, conv_a_w, yb, hf, hb, cg, yd, w_in_b, bg, wbr, wo, l, lat, tile_big, tile)
        x = _ffn(x, mods, ng, wg, wu, wd, l, 1, lat, tile_big, tile)
        if ctx_out:
            yb_c = jnp.swapaxes(_ctx_attention(qp_c, k_c, vb_c, sink, l, True, WIN_Q_COLS), 1, 2)
            yd_c = jnp.swapaxes(_ctx_attention(dq_c, dk_c, dv_c, sink, l, False, (0, 1, 2, 3)), 1, 2)
            xc = _merge(xc, mods, ng, pa_c, ab_c, conv_a_w, yb_c, hf_c, hb_c, cg_c, yd_c,
                        w_in_b, bg, wbr, wo, l, cx_row, tile_x, tile_x)
            xc = _ffn(xc.reshape(1, b * lx, d), mods, ng, wg, wu, wd, l, 1, cx_row,
                      tile_cf, tile).reshape(b, lx, d)
    return x
```

```python
import functools

import jax
import jax.numpy as jnp
import numpy as np
from jax import lax
from jax.experimental import pallas as pl
from jax.experimental.pallas import tpu as pltpu

HEAD_DIM = 64
GRID_W = 64
WINDOW = 128
NA_ROWS = 8
NA_COLS = 16
N_BRANCH = 4
CONV_K = 3
LRU_CONV_K = 4
LRU_C = 8.0
ROPE_BASE = 10000.0
EPS = 1e-6
NEG_INF = -1e30
N_MOD = 9
LOG2E = 1.4426950408889634

V7X_VMEM_LIMIT_BYTES = 60 * 1024 * 1024
HALO = 8
CONV_DELAY = HALO
LANES = 128
MERGE_CHUNK = 512

F32 = jnp.float32
BF16 = jnp.bfloat16


def _params(n_grid):
    return pltpu.CompilerParams(
        dimension_semantics=("arbitrary",) * n_grid,
        vmem_limit_bytes=V7X_VMEM_LIMIT_BYTES)


def _dot(a, b):
    return jnp.dot(a, b, preferred_element_type=F32)


def _dot_nt(a, b):
    return lax.dot_general(a, b, (((1,), (1,)), ((), ())), preferred_element_type=F32)


def _sigmoid(x):
    return jax.nn.sigmoid(x)


def _gelu_tanh(x):
    return 0.5 * x * (1.0 + jnp.tanh(0.7978845608028654 * (x + 0.044715 * (x * x * x))))


def _adaln(x, g, shift, scale):
    ms = jnp.mean(x * x, axis=-1, keepdims=True)
    return (x * lax.rsqrt(ms + EPS) * g) * (1.0 + scale) + shift


def _full(shape):
    n = len(shape)
    return pl.BlockSpec(shape, lambda *_: (0,) * n)


def _slab(shape, *lead):
    n = len(shape)
    return pl.BlockSpec((None,) * len(lead) + tuple(shape), lambda *_: tuple(lead) + (0,) * n,
                        pipeline_mode=pl.Buffered(1))


def _mod_spec(d, layer, ctx_row):
    if ctx_row is None:
        return pl.BlockSpec((None, None, N_MOD, d), lambda bi, i: (layer, bi, 0, 0))
    return pl.BlockSpec((None, None, N_MOD, d), lambda bi, i: (layer, ctx_row, 0, 0))


def _mod_kernel(ct_ref, w_ref, b_ref, o_ref, *, n_rows):
    ct = ct_ref[...]
    st = ct * _sigmoid(ct)
    w = w_ref[...]
    rows = [jnp.sum(w * st[:, r:r + 1], axis=0, keepdims=True) for r in range(n_rows)]
    rows += [jnp.zeros_like(rows[0])] * (o_ref.shape[0] - n_rows)
    o_ref[...] = jnp.concatenate(rows, axis=0) + b_ref[...]


def _modulation(cs_t, n_rows, w_mod, b_mod):
    depth, d, n = w_mod.shape
    tn = n // 8
    return pl.pallas_call(
        functools.partial(_mod_kernel, n_rows=n_rows),
        grid=(depth, n // tn),
        in_specs=[pl.BlockSpec((d, 8), lambda l, j: (0, 0)),
                  pl.BlockSpec((None, d, tn), lambda l, j: (l, 0, j)),
                  pl.BlockSpec((None, 1, tn), lambda l, j: (l, 0, j))],
        out_specs=pl.BlockSpec((None, 8, tn), lambda l, j: (l, 0, j)),
        out_shape=jax.ShapeDtypeStruct((depth, 8, n), F32),
        compiler_params=_params(2),
        name="modulation",
    )(cs_t, w_mod, b_mod.reshape(depth, 1, n))


def _ffn_kernel(x_ref, mod_ref, g_ref, wg_ref, wu_ref, wd_ref, o_ref, *, row0, chunks, sub):
    for r0 in range(0, x_ref.shape[0], sub):
        rows = slice(r0, r0 + sub)
        x = x_ref[rows, :]
        h = _adaln(x, g_ref[...], mod_ref[row0:row0 + 1, :], mod_ref[row0 + 1:row0 + 2, :]).astype(BF16)
        acc = None
        for (lo, hi) in chunks:
            a = _dot(h, wg_ref[:, lo:hi])
            u = _dot(h, wu_ref[:, lo:hi])
            act = (a * _sigmoid(a) * u).astype(BF16)
            part = _dot(act, wd_ref[lo:hi, :])
            acc = part if acc is None else acc + part
        o_ref[rows, :] = x + (0.5 * mod_ref[row0 + 2:row0 + 3, :]) * acc


def _ffn(x, mods, norm_g, wg, wu, wd, layer, which, ctx_row, tile, sub):
    b, s, d = x.shape
    f = wg.shape[3]
    half = (f // 2 + 255) // 256 * 256
    chunks = ((0, half), (half, f)) if half < f else ((0, f),)
    return pl.pallas_call(
        functools.partial(_ffn_kernel, row0=6 * which, chunks=chunks, sub=min(sub, tile)),
        grid=(b, s // tile),
        in_specs=[pl.BlockSpec((None, tile, d), lambda bi, i: (bi, i, 0)),
                  _mod_spec(d, layer, ctx_row),
                  _slab((1, d), layer, 2 * which),
                  _slab((d, f), layer, which), _slab((d, f), layer, which), _slab((f, d), layer, which)],
        out_specs=pl.BlockSpec((None, tile, d), lambda bi, i: (bi, i, 0)),
        out_shape=jax.ShapeDtypeStruct((b, s, d), F32),
        compiler_params=_params(2),
        name="ffn",
    )(x, mods, norm_g, wg, wu, wd)


def _head_rms(t, m_ref, g):
    n = t.shape[1]
    sq = t * t
    hi = sq.astype(BF16)
    lo = (sq - hi.astype(F32)).astype(BF16)
    m = m_ref[0:n, 0:n]
    ms = _dot(hi, m) + _dot(lo, m)
    return t * lax.rsqrt(ms + EPS) * g


def _rope(t, cos, sin_signed):
    n = t.shape[1]
    lane = lax.broadcasted_iota(jnp.int32, t.shape, 1)
    partner = jnp.where((lane % 32) < 16, pltpu.roll(t, n - 16, 1), pltpu.roll(t, 16, 1))
    return t * cos + partner * sin_signed


def _inproj_kernel(x_ref, mod_ref, g_ref, w_ref, qkg_ref, m_ref, cos_ref, sin_ref, cw_ref, cb_ref,
                   pa_ref, ab_ref, qr_ref, qp_ref, kr_ref, vb_ref, us_ref, ut_ref, cg_ref,
                   dq_ref, dk_ref, dv_ref, tail_ref, *, mw, sub):
    i = pl.program_id(1)
    nt = pl.num_programs(1)
    t = x_ref.shape[0]
    kvw = mw // 2
    scale = LOG2E * HEAD_DIM ** -0.5

    @pl.when(i == 0)
    def _():
        tail_ref[...] = jnp.zeros_like(tail_ref)

    cx_groups = []
    for r0 in range(0, t, sub):
        rows = slice(r0, r0 + sub)
        h = _adaln(x_ref[rows, :], g_ref[...], mod_ref[3:4, :], mod_ref[4:5, :]).astype(BF16)
        u = _dot(h, w_ref[...])
        o = 0
        ax = u[:, o:o + mw]; o += mw
        ab = u[:, o:o + mw]; o += mw
        ac = u[:, o:o + mw]; o += mw
        bq = u[:, o:o + mw]; o += mw
        qa, qb = bq[:, :LANES], bq[:, LANES:]
        low = _half_mask(qa.shape, True)
        bq = jnp.concatenate([jnp.where(low, qa, pltpu.roll(qb, HEAD_DIM, 1)),
                              jnp.where(low, pltpu.roll(qa, HEAD_DIM, 1), qb)], axis=1)
        bk = u[:, o:o + kvw]; o += kvw
        bv = u[:, o:o + kvw]; o += kvw
        cx = u[:, o:o + mw]; o += mw
        cg = u[:, o:o + mw]; o += mw
        dq = u[:, o:o + mw]; o += mw
        dk = u[:, o:o + mw]; o += mw
        dv = u[:, o:o + mw]; o += mw
        pa_ref[rows, :] = ac * ax
        ab_ref[rows, :] = ab
        cx_groups.append(cx)
        cg_ref[rows, :] = cg
        cos = cos_ref[rows, :]
        sin = sin_ref[rows, :]
        cos2 = jnp.concatenate([cos, cos], axis=1)
        sin2 = jnp.concatenate([sin, sin], axis=1)
        q = _head_rms(bq, m_ref, qkg_ref[0:1, :]) * scale
        qp_ref[rows, :] = q.astype(BF16)
        qr_ref[rows, :] = _rope(q, cos2, sin2).astype(BF16)
        k = _head_rms(bk, m_ref, qkg_ref[1:2, 0:kvw])
        kr_ref[rows, :] = _rope(k, cos, sin).astype(BF16)
        vb_ref[rows, :] = bv.astype(BF16)
        dq_ref[rows, :] = (_head_rms(dq, m_ref, qkg_ref[2:3, :]) * scale).astype(BF16)
        dk_ref[rows, :] = _head_rms(dk, m_ref, qkg_ref[3:4, :]).astype(BF16)
        dv_ref[rows, :] = dv.astype(BF16)

    def conv(ext, n):
        lead = 2 * CONV_DELAY - CONV_DELAY - LRU_CONV_K // 2
        acc = cb_ref[...]
        for k in range(LRU_CONV_K):
            acc = acc + ext[lead + k:lead + k + n] * cw_ref[k:k + 1, :]
        return acc

    cx_tile = jnp.concatenate(cx_groups, axis=0)
    us_ref[...] = conv(jnp.concatenate([tail_ref[...], cx_tile], axis=0), t)
    tail = cx_tile[t - 2 * CONV_DELAY:, :]
    tail_ref[...] = tail

    @pl.when(i == nt - 1)
    def _():
        ut_ref[...] = conv(jnp.concatenate([tail, jnp.zeros((CONV_DELAY, mw), F32)], axis=0), CONV_DELAY)


def _inproj(x, mods, norm_g, w_in, qkg, mhead, cos, sin, cw, cb, layer, ctx_row, tile, sub):
    b, s, d = x.shape
    mw = qkg.shape[2]
    kvw = mw // 2
    nw = 9 * mw + 2 * kvw
    assert mw == 2 * LANES and tile >= 2 * CONV_DELAY
    tok = lambda width: pl.BlockSpec((None, tile, width), lambda bi, i: (bi, i, 0))
    end = pl.BlockSpec((None, CONV_DELAY, mw), lambda bi, i: (bi, 0, 0))
    widths = (mw, mw, mw, mw, kvw, kvw, mw, None, mw, mw, mw, mw)
    dtypes = (F32, F32, BF16, BF16, BF16, BF16, F32, F32, F32, BF16, BF16, BF16)
    return pl.pallas_call(
        functools.partial(_inproj_kernel, mw=mw, sub=min(sub, tile)),
        grid=(b, s // tile),
        in_specs=[tok(d),
                  _mod_spec(d, layer, ctx_row),
                  _slab((1, d), layer, 1), _slab((d, nw), layer), _slab((4, mw), layer), _full((mw, mw)),
                  pl.BlockSpec((tile, kvw), lambda bi, i: (i, 0)),
                  pl.BlockSpec((tile, kvw), lambda bi, i: (i, 0)),
                  _slab((LRU_CONV_K, mw), layer), _slab((1, mw), layer)],
        out_specs=[end if wd is None else tok(wd) for wd in widths],
        out_shape=[jax.ShapeDtypeStruct((b, CONV_DELAY, mw) if wd is None else (b, s, wd), dt)
                   for wd, dt in zip(widths, dtypes)],
        scratch_shapes=[pltpu.VMEM((2 * CONV_DELAY, mw), F32)],
        compiler_params=_params(2),
        name="inproj",
    )(x, mods, norm_g, w_in, qkg, mhead, cos, sin, cw, cb)


def _shift_rows(t, d, fill, reverse):
    n = t.shape[0]
    row = lax.broadcasted_iota(jnp.int32, t.shape, 0)
    if reverse:
        return jnp.where(row < n - d, pltpu.roll(t, n - d, 0), fill)
    return jnp.where(row >= d, pltpu.roll(t, d, 0), fill)


def _lru_coeffs(cur_ref, next_ref, end_ref, is_last, wa_ref, ba_ref, wx_ref, bx_ref, lam_ref, direction):
    follow = jnp.where(is_last, end_ref[...], next_ref[...])
    u = jnp.concatenate([cur_ref[CONV_DELAY:, :], follow], axis=0)
    ub = u.astype(BF16)
    r = _sigmoid(_dot(ub, wa_ref[direction]) + ba_ref[direction:direction + 1, :])
    ig = _sigmoid(_dot(ub, wx_ref[direction]) + bx_ref[direction:direction + 1, :])
    nl = -lam_ref[direction:direction + 1, :]
    softplus = jnp.maximum(nl, 0.0) + jnp.log1p(jnp.exp(-jnp.abs(nl)))
    neg_log_a = r * (LRU_C * softplus)
    a = jnp.exp2(r * ((-LRU_C * LOG2E) * softplus))
    z = jnp.tanh(neg_log_a) * (a * a + 1.0)
    return a, jnp.where(z > 0.0, z * lax.rsqrt(z), 0.0) * (ig * u)


def _scan_tile(a, bb, carry, reverse, sa_ref, sb_ref, sc_ref, out_ref):
    t, w = a.shape
    g = t // HALO
    a3 = a.reshape(g, HALO, w)
    b3 = bb.reshape(g, HALO, w)
    sub = lax.broadcasted_iota(jnp.int32, (g, HALO, w), 1)
    d = 1
    while d < HALO:
        ok = (sub < HALO - d) if reverse else (sub >= d)
        shift = HALO - d if reverse else d
        a_r = pltpu.roll(a3, shift, 1)
        b_r = pltpu.roll(b3, shift, 1)
        b3 = jnp.where(ok, a3 * b_r + b3, b3)
        a3 = jnp.where(ok, a3 * a_r, a3)
        d *= 2
    sa_ref[...] = a3.reshape(t, w)
    sb_ref[...] = b3.reshape(t, w)
    edge = 0 if reverse else HALO - 1
    at = sa_ref[pl.ds(edge, g, stride=HALO), :]
    bt = sb_ref[pl.ds(edge, g, stride=HALO), :]
    d = 1
    while d < g:
        a_s = _shift_rows(at, d, 1.0, reverse)
        b_s = _shift_rows(bt, d, 0.0, reverse)
        bt = at * b_s + bt
        at = at * a_s
        d *= 2
    h_end = at * carry + bt
    sc_ref[...] = _shift_rows(h_end, 1, carry, reverse)
    for gi in range(g):
        rows = slice(gi * HALO, (gi + 1) * HALO)
        out_ref[rows, :] = sa_ref[rows, :] * sc_ref[gi:gi + 1, :] + sb_ref[rows, :]
    return h_end[0:1, :] if reverse else h_end[g - 1:g, :]


def _scan_kernel(fc_ref, fn_ref, rc_ref, rn_ref, end_ref, h0_ref,
                 wa_ref, ba_ref, wx_ref, bx_ref, lam_ref, hf_ref, hb_ref,
                 carry_ref, sa_ref, sb_ref, sc_ref):
    i = pl.program_id(1)
    nt = pl.num_programs(1)

    @pl.when(i == 0)
    def _():
        carry_ref[...] = h0_ref[...]

    par = (wa_ref, ba_ref, wx_ref, bx_ref, lam_ref)
    tiles = ((fc_ref, fn_ref, i == nt - 1, hf_ref), (rc_ref, rn_ref, i == 0, hb_ref))
    for direction, (c_ref, n_ref, is_last, out_ref) in enumerate(tiles):
        a, bb = _lru_coeffs(c_ref, n_ref, end_ref, is_last, *par, direction)
        for j in range(a.shape[1] // LANES):
            cols = slice(j * LANES, (j + 1) * LANES)
            carry_ref[direction:direction + 1, cols] = _scan_tile(
                a[:, cols], bb[:, cols], carry_ref[direction:direction + 1, cols], direction == 1,
                sa_ref.at[direction, j], sb_ref.at[direction, j], sc_ref.at[direction, j],
                out_ref.at[:, cols])


def _halo_specs(tile, width, n_rows, tile_index):
    r = tile // HALO
    last = n_rows // HALO - 1
    prev = pl.BlockSpec((None, HALO, width),
                        lambda bi, i: (bi, jnp.maximum(tile_index(i) * r - 1, 0), 0))
    cur = pl.BlockSpec((None, tile, width), lambda bi, i: (bi, tile_index(i), 0))
    nxt = pl.BlockSpec((None, HALO, width),
                       lambda bi, i: (bi, jnp.minimum((tile_index(i) + 1) * r, last), 0))
    return [prev, cur, nxt]


def _scan(us, ut, h0, wa, ba, wx, bx, lam, layer, tile):
    b, s, w = us.shape
    nt = s // tile
    fwd = lambda i: i
    rev = lambda i: nt - 1 - i
    return pl.pallas_call(
        _scan_kernel,
        grid=(b, nt),
        in_specs=(_halo_specs(tile, w, s, fwd)[1:] + _halo_specs(tile, w, s, rev)[1:]
                  + [pl.BlockSpec((None, CONV_DELAY, w), lambda bi, i: (bi, 0, 0)),
                     pl.BlockSpec((None, 2, w), lambda bi, i: (bi, 0, 0)),
                     _slab((2, w, w), layer), _slab((2, w), layer),
                     _slab((2, w, w), layer), _slab((2, w), layer), _slab((2, w), layer)]),
        out_specs=[pl.BlockSpec((None, tile, w), lambda bi, i: (bi, i, 0)),
                   pl.BlockSpec((None, tile, w), lambda bi, i: (bi, nt - 1 - i, 0))],
        out_shape=[jax.ShapeDtypeStruct((b, s, w), F32)] * 2,
        scratch_shapes=[pltpu.VMEM((2, w), F32),
                        pltpu.VMEM((2, w // LANES, tile, LANES), F32),
                        pltpu.VMEM((2, w // LANES, tile, LANES), F32),
                        pltpu.VMEM((2, w // LANES, tile // HALO, LANES), F32)],
        compiler_params=_params(2),
        name="lru_scan",
    )(us, us, us, us, ut, h0, wa, ba, wx, bx, lam)


def _softmax_pv(parts, sink):
    m = None
    for s, _ in parts:
        mx = jnp.max(s, axis=-1, keepdims=True)
        m = mx if m is None else jnp.maximum(m, mx)
    if sink is not None:
        m = jnp.maximum(m, sink)
    den = None
    out = None
    for s, v in parts:
        p = jnp.exp2(s - m)
        sm = jnp.sum(p, axis=-1, keepdims=True)
        den = sm if den is None else den + sm
        pv = _dot(p.astype(BF16), v)
        out = pv if out is None else out + pv
    if sink is not None:
        den = den + jnp.exp2(sink - m)
    return out / den


def _dot_tn(a, b):
    return lax.dot_general(a, b, (((0,), (0,)), ((), ())), preferred_element_type=F32)


def _half_mask(shape, lower):
    lane = lax.broadcasted_iota(jnp.int32, shape, 1) % LANES
    return (lane < HEAD_DIM) if lower else (lane >= HEAD_DIM)


def _softmax_pv_t(parts, sink_row, lower):
    m = None
    for s, _ in parts:
        mx = jnp.max(s, axis=0, keepdims=True)
        m = mx if m is None else jnp.maximum(m, mx)
    if sink_row is not None:
        m = jnp.maximum(m, sink_row)
    res = None
    for s, v in parts:
        r = _dot_tn(v, jnp.exp2(s - m).astype(BF16))
        res = r if res is None else res + r
    out, den = (res[:HEAD_DIM], res[HEAD_DIM:HEAD_DIM + 1]) if lower else (res[HEAD_DIM:], res[0:1])
    if sink_row is not None:
        den = den + jnp.exp2(sink_row - m)
    return out * (1.0 / den)


WIN_Q_COLS = (0, 2, 1, 3)


def _window_kernel(qr_ref, qp_ref, kp_ref, kc_ref, kn_ref, vp_ref, vc_ref, vn_ref,
                   kx_ref, vx_ref, sink_ref, band_ref, o_ref):
    i = pl.program_id(1)
    nt = pl.num_programs(1)
    tq = qr_ref.shape[0]
    nblk = tq // WINDOW
    qr = qr_ref[...]
    qp = qp_ref[...]
    kcat = jnp.concatenate([kp_ref[...], kc_ref[...], kn_ref[...]], axis=0)
    vcat = jnp.concatenate([vp_ref[...], vc_ref[...], vn_ref[...]], axis=0)
    kx = kx_ref[...]
    vx = vx_ref[...]
    for hk in range(2):
        lower = hk == 0
        qm = _half_mask((tq, LANES), lower)
        zero = jnp.zeros((tq, LANES), BF16)
        qra, qrb = jnp.where(qm, qr[:, :LANES], zero), jnp.where(qm, qr[:, LANES:], zero)
        qpa, qpb = jnp.where(qm, qp[:, :LANES], zero), jnp.where(qm, qp[:, LANES:], zero)
        vw = jnp.where(_half_mask(vcat.shape, lower), vcat, jnp.ones_like(vcat))
        vxw = jnp.where(_half_mask(vx.shape, lower), vx, jnp.ones_like(vx))
        s_ctx = _dot_nt(kx, jnp.concatenate([qpa, qpb], axis=0))
        for j in range(nblk):
            rq = slice(j * WINDOW, (j + 1) * WINDOW)
            rk = slice(j * WINDOW, (j + 3) * WINDOW)
            if j == 0:
                band = band_ref[jnp.where(i == 0, 1, 0)]
            elif j == nblk - 1:
                band = band_ref[jnp.where(i == nt - 1, 2, 0)]
            else:
                band = band_ref[0]
            s_loc = (_dot_nt(kcat[rk], jnp.concatenate([qra[rq], qrb[rq]], axis=0))
                     + jnp.concatenate([band, band], axis=1))
            s_cx = jnp.concatenate([s_ctx[:, rq], s_ctx[:, tq + j * WINDOW:tq + (j + 1) * WINDOW]], axis=1)
            o = _softmax_pv_t([(s_loc, vw[rk]), (s_cx, vxw)], sink_ref[hk], lower).astype(o_ref.dtype)
            ha, hb = 2 * hk, 2 * hk + 1
            o_ref[ha * HEAD_DIM:(ha + 1) * HEAD_DIM, rq] = o[:, :WINDOW]
            o_ref[hb * HEAD_DIM:(hb + 1) * HEAD_DIM, rq] = o[:, WINDOW:]


def _window_band():
    kk = np.arange(3 * WINDOW)[:, None]
    r = np.arange(WINDOW)[None, :]
    ok = np.abs(kk - WINDOW - r) <= WINDOW
    variants = [ok, ok & (kk >= WINDOW), ok & (kk < 2 * WINDOW)]
    return jnp.asarray(np.where(np.stack(variants), 0.0, NEG_INF).astype(np.float32))


def _window_attention(qr, qp, k, v, kx, vx, sink_rows, layer, tile):
    b, s, qw = qr.shape
    kw = k.shape[2]
    lx = kx.shape[1]
    assert kw == LANES and qw == 2 * LANES and tile % WINDOW == 0 and s >= 2 * WINDOW
    r = tile // WINDOW
    last = s // WINDOW - 1
    band = _window_band()
    qspec = pl.BlockSpec((None, tile, qw), lambda bi, i: (bi, i, 0))
    prev = pl.BlockSpec((None, WINDOW, kw), lambda bi, i: (bi, jnp.maximum(i * r - 1, 0), 0))
    cur = pl.BlockSpec((None, tile, kw), lambda bi, i: (bi, i, 0))
    nxt = pl.BlockSpec((None, WINDOW, kw), lambda bi, i: (bi, jnp.minimum((i + 1) * r, last), 0))
    ctx = pl.BlockSpec((None, lx, kw), lambda bi, i: (bi, 0, 0))
    return pl.pallas_call(
        _window_kernel,
        grid=(b, s // tile),
        in_specs=[qspec, qspec, prev, cur, nxt, prev, cur, nxt, ctx, ctx,
                  _slab(sink_rows.shape[1:], layer), _full(band.shape)],
        out_specs=pl.BlockSpec((None, qw, tile), lambda bi, i: (bi, 0, i)),
        out_shape=jax.ShapeDtypeStruct((b, qw, s), BF16),
        compiler_params=_params(2),
        name="window_attention",
    )(qr, qp, k, k, k, v, v, v, kx, vx, sink_rows, band)


NBR_MASKED = 2 * NA_ROWS - 1


def _nbr_kernel(q_ref, kp_ref, kc_ref, kn_ref, vp_ref, vc_ref, vn_ref, kx_ref, vx_ref, sl_ref, sr_ref,
                o_ref, bias_ref, *, slab_index):
    i = pl.program_id(1)
    nt = pl.num_programs(1)
    tq = kp_ref.shape[0]
    n_sub = q_ref.shape[0] // tq
    tile_rows = tq // GRID_W

    @pl.when((pl.program_id(0) == 0) & (i == 0))
    def _():
        for v in range(3):
            for g in range(q_ref.shape[1] // LANES):
                for bj in range(3 * tile_rows):
                    for e in range(2):
                        for pp in range(tile_rows // 2):
                            left = slab_index[v][2 * pp][bj // tile_rows][bj % tile_rows]
                            right = slab_index[v][2 * pp + 1][bj // tile_rows][bj % tile_rows]
                            bias_ref[v, g, bj * GRID_W:(bj + 1) * GRID_W,
                                     e * tq + pp * LANES:e * tq + (pp + 1) * LANES] = (
                                sl_ref[2 * g + e, left] + sr_ref[2 * g + e, right])

    kcat = jnp.concatenate([kp_ref[...], kc_ref[...], kn_ref[...]], axis=0)
    vcat = jnp.concatenate([vp_ref[...], vc_ref[...], vn_ref[...]], axis=0)
    kx = kx_ref[...]
    vx = vx_ref[...]
    for sub in range(n_sub):
        variant = 1
        if sub == 0:
            variant = jnp.where(i == 0, 0, variant)
        if sub == n_sub - 1:
            variant = jnp.where(i == nt - 1, 2, variant)
        q = q_ref[sub * tq:(sub + 1) * tq, :]
        rk = slice(sub * tq, (sub + 3) * tq)
        for g in range(q.shape[1] // LANES):
            cols = slice(g * LANES, (g + 1) * LANES)
            qg, vg, vxg = q[:, cols], vcat[rk, cols], vx[:, cols]
            low = _half_mask(qg.shape, True)
            zero = jnp.zeros_like(qg)
            qcat = jnp.concatenate([jnp.where(low, qg, zero), jnp.where(low, zero, qg)], axis=0)
            s_loc = _dot_nt(kcat[rk, cols], qcat) + bias_ref[variant, g]
            s_ctx = _dot_nt(kx[:, cols], qcat)
            for e in range(2):
                lower = e == 0
                qs = slice(e * tq, (e + 1) * tq)
                vw = jnp.where(_half_mask(vg.shape, lower), vg, jnp.ones_like(vg))
                vxw = jnp.where(_half_mask(vxg.shape, lower), vxg, jnp.ones_like(vxg))
                o = _softmax_pv_t([(s_loc[:, qs], vw), (s_ctx[:, qs], vxw)], None, lower)
                h = 2 * g + e
                o_ref[h * HEAD_DIM:(h + 1) * HEAD_DIM, sub * tq:(sub + 1) * tq] = o.astype(o_ref.dtype)


def _nbr_slab_index(rows, tile_rows):
    n_tiles = rows // tile_rows
    index = []
    for ti in (0, 1, n_tiles - 1):
        r = ti * tile_rows + np.arange(tile_rows)
        start = np.clip(r - NA_ROWS // 2, 0, rows - NA_ROWS)
        blk = np.arange(3)[:, None] - 1 + ti
        krow = blk * tile_rows + np.arange(tile_rows)[None, :]
        ok = ((krow[None] >= start[:, None, None]) & (krow[None] < start[:, None, None] + NA_ROWS)
              & ((blk >= 0) & (blk < n_tiles))[None])
        dr = krow[None] - r[:, None, None] + NA_ROWS - 1
        assert ((dr >= 0) & (dr < NBR_MASKED))[ok].all()
        index.append(np.where(ok, dr, NBR_MASKED))
    return np.stack(index).tolist()


def _nbr_slabs(rel_bias):
    qc = np.arange(GRID_W)[None, :]
    kc = np.arange(GRID_W)[:, None]
    dc = np.clip(kc - qc, 1 - NA_COLS, NA_COLS - 1) + NA_COLS - 1
    c_start = np.clip(qc - NA_COLS // 2, 0, GRID_W - NA_COLS)
    col_ok = (kc >= c_start) & (kc < c_start + NA_COLS)
    one_hot = jnp.asarray(np.eye(2 * NA_COLS - 1, dtype=np.float32)[dc])
    slabs = jnp.einsum('kqc,lhrc->lhrkq', one_hot, rel_bias.astype(F32), precision=lax.Precision.HIGHEST)
    slabs = jnp.where(jnp.asarray(col_ok), slabs, NEG_INF)
    slabs = jnp.concatenate([slabs, jnp.full_like(slabs[:, :, :1], NEG_INF)], axis=2)
    zeros = jnp.zeros_like(slabs)
    return jnp.concatenate([slabs, zeros], axis=-1), jnp.concatenate([zeros, slabs], axis=-1)


def _nbr_attention(q, k, v, kx, vx, slabs, slab_index, layer, tile, n_sub):
    b, s, w = q.shape
    lx = kx.shape[1]
    step = n_sub * tile
    nt = s // step
    last = s // tile - 1
    left, right = slabs
    tok = pl.BlockSpec((None, step, w), lambda bi, i: (bi, i, 0))
    prev = pl.BlockSpec((None, tile, w), lambda bi, i: (bi, jnp.maximum(i * n_sub - 1, 0), 0))
    nxt = pl.BlockSpec((None, tile, w), lambda bi, i: (bi, jnp.minimum((i + 1) * n_sub, last), 0))
    ctx = pl.BlockSpec((None, lx, w), lambda bi, i: (bi, 0, 0))
    return pl.pallas_call(
        functools.partial(_nbr_kernel, slab_index=slab_index),
        grid=(b, nt),
        in_specs=[tok, prev, tok, nxt, prev, tok, nxt, ctx, ctx,
                  _slab(left.shape[1:], layer), _slab(right.shape[1:], layer)],
        out_specs=pl.BlockSpec((None, w, step), lambda bi, i: (bi, 0, i)),
        out_shape=jax.ShapeDtypeStruct((b, w, s), BF16),
        scratch_shapes=[pltpu.VMEM((3, w // LANES, 3 * tile, 2 * tile), F32)],
        compiler_params=_params(2),
        name="nbr_attention",
    )(q, k, k, k, v, v, v, kx, vx, left, right)


def _ctx_attn_kernel(q_ref, k_ref, v_ref, sink_ref, o_ref, *, n_kv, group, use_sink, q_cols):
    q = q_ref[...]
    k = k_ref[...]
    v = v_ref[...]
    l = q.shape[0]
    outs = []
    for hk in range(n_kv):
        ks = slice(hk * HEAD_DIM, (hk + 1) * HEAD_DIM)
        for h in range(hk * group, (hk + 1) * group):
            s = _dot_nt(q[:, q_cols[h] * HEAD_DIM:(q_cols[h] + 1) * HEAD_DIM], k[:, ks])
            sink = jnp.broadcast_to(sink_ref[0:1, h:h + 1], (l, 1)) if use_sink else None
            outs.append(_softmax_pv([(s, v[:, ks])], sink))
    o_ref[...] = jnp.concatenate(outs, axis=1).astype(o_ref.dtype)


def _ctx_attention(q, k, v, sink, layer, use_sink, q_cols):
    b, l, qw = q.shape
    kw = k.shape[2]
    n_kv = kw // HEAD_DIM
    group = (qw // HEAD_DIM) // n_kv
    return pl.pallas_call(
        functools.partial(_ctx_attn_kernel, n_kv=n_kv, group=group, use_sink=use_sink, q_cols=q_cols),
        grid=(b,),
        in_specs=[pl.BlockSpec((None, l, qw), lambda bi: (bi, 0, 0)),
                  pl.BlockSpec((None, l, kw), lambda bi: (bi, 0, 0)),
                  pl.BlockSpec((None, l, kw), lambda bi: (bi, 0, 0)),
                  _slab(sink.shape[1:], layer)],
        out_specs=pl.BlockSpec((None, l, qw), lambda bi: (bi, 0, 0)),
        out_shape=jax.ShapeDtypeStruct((b, l, qw), BF16),
        compiler_params=_params(1),
        name="ctx_attention",
    )(q, k, v, sink)


def _merge_kernel(x_ref, mod_ref, g_ref, pp_ref, pc_ref, pn_ref, ab_ref, caw_ref,
                  yb_ref, hf_ref, hb_ref, cg_ref, yd_ref,
                  bg_ref, wbr_ref, wo_ref, *rest, sub):
    wgl_refs, o_ref = rest[:-1], rest[-1]
    i = pl.program_id(1)
    nt = pl.num_programs(1)
    t, d = x_ref.shape
    prev = jnp.where(i > 0, pp_ref[...], 0.0)
    nxt = jnp.where(i < nt - 1, pn_ref[...], 0.0)
    ext = jnp.concatenate([prev, pc_ref[...], nxt], axis=0)
    for r0 in range(0, t, sub):
        rows = slice(r0, r0 + sub)
        x = x_ref[rows, :]
        h = _adaln(x, g_ref[...], mod_ref[3:4, :], mod_ref[4:5, :]).astype(BF16)
        conv = (ext[HALO - 1 + r0:HALO - 1 + r0 + sub] * caw_ref[0:1, :]
                + ext[HALO + r0:HALO + r0 + sub] * caw_ref[1:2, :]
                + ext[HALO + 1 + r0:HALO + 1 + r0 + sub] * caw_ref[2:3, :])
        y_a = (ab_ref[rows, :] * conv).astype(BF16)
        y_c = ((hf_ref[rows, :] + hb_ref[rows, :]) * _gelu_tanh(cg_ref[rows, :])).astype(BF16)
        ys = ((y_a, _dot), (yb_ref[:, rows], _dot_tn), (y_c, _dot), (yd_ref[:, rows], _dot_tn))
        out = None
        for c0 in range(0, d, MERGE_CHUNK):
            merged = None
            for n, (y, dot) in enumerate(ys):
                gc = slice(n * d + c0, n * d + c0 + MERGE_CHUNK)
                gate = _sigmoid(_dot(h, wgl_refs[(n * d + c0) // MERGE_CHUNK][...]) + bg_ref[0:1, gc])
                term = gate * dot(y, wbr_ref[n, :, c0:c0 + MERGE_CHUNK])
                merged = term if merged is None else merged + term
            part = _dot(merged.astype(BF16), wo_ref[c0:c0 + MERGE_CHUNK, :])
            out = part if out is None else out + part
        o_ref[rows, :] = x + mod_ref[5:6, :] * out


def _merge(x, mods, norm_g, pa, ab, caw, yb, hf, hb, cg, yd, w_in, bg, wbr, wo, layer, ctx_row, tile, sub):
    b, s, d = x.shape
    mw = pa.shape[2]
    n_gate = N_BRANCH * d
    first, rem = divmod(w_in.shape[2] - n_gate, MERGE_CHUNK)
    assert rem == 0 and d % MERGE_CHUNK == 0
    gate_specs = [pl.BlockSpec((None, d, MERGE_CHUNK), lambda *_, k=k: (layer, 0, first + k),
                               pipeline_mode=pl.Buffered(1)) for k in range(n_gate // MERGE_CHUNK)]
    tok = lambda width: pl.BlockSpec((None, tile, width), lambda bi, i: (bi, i, 0))
    tok_t = pl.BlockSpec((None, mw, tile), lambda bi, i: (bi, 0, i))
    return pl.pallas_call(
        functools.partial(_merge_kernel, sub=min(sub, tile)),
        grid=(b, s // tile),
        in_specs=([tok(d), _mod_spec(d, layer, ctx_row), _slab((1, d), layer, 1)]
                  + _halo_specs(tile, mw, s, lambda i: i)
                  + [tok(mw), _slab((CONV_K, mw), layer),
                     tok_t, tok(mw), tok(mw), tok(mw), tok_t,
                     _slab(bg.shape[1:], layer), _slab(wbr.shape[1:], layer), _slab(wo.shape[1:], layer)]
                  + gate_specs),
        out_specs=tok(d),
        out_shape=jax.ShapeDtypeStruct((b, s, d), F32),
        compiler_params=_params(2),
        name="merge",
    )(x, mods, norm_g, pa, pa, pa, ab, caw, yb, hf, hb, cg, yd, bg, wbr, wo, *([w_in] * len(gate_specs)))


def _rope_tables(n_tok, width):
    half = HEAD_DIM // 2
    nf = half // 2
    inv_freq = ROPE_BASE ** (-np.arange(nf, dtype=np.float64) / nf)
    pos = np.arange(n_tok)
    ang_r = (pos // GRID_W).astype(np.float64)[:, None] * inv_freq[None, :]
    ang_c = (pos % GRID_W).astype(np.float64)[:, None] * inv_freq[None, :]
    cos = np.concatenate([np.cos(ang_r)] * 2 + [np.cos(ang_c)] * 2, axis=1)
    sin = np.concatenate([-np.sin(ang_r), np.sin(ang_r), -np.sin(ang_c), np.sin(ang_c)], axis=1)
    reps = width // HEAD_DIM
    return (jnp.asarray(np.tile(cos, (1, reps)).astype(np.float32)),
            jnp.asarray(np.tile(sin, (1, reps)).astype(np.float32)))


def _block_diag(w):
    *lead, n, c, e = w.shape
    eye = jnp.eye(n, dtype=w.dtype)
    return (eye[:, None, :, None] * w[..., :, :, None, :]).reshape(*lead, n * c, n * e)


def _pick_tile(n, target):
    t = min(n, target)
    while n % t:
        t //= 2
    return t


def kernel(x, c, ctx, c_ctx, w_mod, b_mod, norm_g, ffn_w_gate, ffn_w_up, ffn_w_down, w_in, b_gate,
           conv_a_w, qk_norm_g, attn_sink, lru_conv_w, lru_conv_b, lru_w_a, lru_b_a, lru_w_x, lru_b_x,
           lru_lam, na_rel_bias, w_branch, w_out):
    b, s, d = x.shape
    lx = ctx.shape[1]
    depth = w_mod.shape[0]
    mw = d // 4
    kvw = mw // 2
    rows = s // GRID_W
    assert s % GRID_W == 0 and rows >= 4 * NA_ROWS and lx % HALO == 0

    tile = _pick_tile(s, 512)
    tile_big = _pick_tile(s, 1024)
    tile_x = _pick_tile(lx, 512)
    tile_cf = _pick_tile(b * lx, 1024)
    tile_win = _pick_tile(s, 2048)
    nbr_rows = 4
    tile_nbr = nbr_rows * GRID_W
    nbr_sub = max(1, _pick_tile(s, 2048) // tile_nbr)

    assert b + 1 <= 8
    cs_t = jnp.zeros((d, 8), F32).at[:, :b].set(c.T).at[:, b].set(c_ctx)
    mods = _modulation(cs_t, b + 1, w_mod, b_mod).reshape(depth, 8, N_MOD, d)
    lat, cx_row = None, b

    cos, sin = _rope_tables(s, kvw)
    cos_x, sin_x = jnp.ones((lx, kvw), F32), jnp.zeros((lx, kvw), F32)
    lane = np.arange(mw)
    mhead = jnp.asarray((lane[:, None] // HEAD_DIM == lane[None, :] // HEAD_DIM) / HEAD_DIM, dtype=BF16)

    ng = norm_g.reshape(depth, 3, 1, d)
    wg, wu, wd = ffn_w_gate.astype(BF16), ffn_w_up.astype(BF16), ffn_w_down.astype(BF16)
    w_in_b = w_in.astype(BF16)
    qkg = jnp.tile(qk_norm_g, (1, 1, mw // HEAD_DIM))
    sink = (attn_sink * LOG2E).reshape(depth, 1, -1)
    sink_rows = jnp.repeat(sink.reshape(depth, 2, 1, 2), WINDOW, axis=3)
    conv_par = (lru_conv_w, lru_conv_b.reshape(depth, 1, mw))
    lru_par = (_block_diag(lru_w_a).astype(BF16), lru_b_a, _block_diag(lru_w_x).astype(BF16), lru_b_x, lru_lam)
    slabs = _nbr_slabs(na_rel_bias * LOG2E)
    slab_index = _nbr_slab_index(rows, nbr_rows)
    bg = b_gate.reshape(depth, 1, -1)
    wbr = w_branch.astype(BF16)
    wo = w_out.astype(BF16)

    xc = ctx
    for l in range(depth):
        ctx_out = l < depth - 1
        x = _ffn(x, mods, ng, wg, wu, wd, l, 0, lat, tile_big, tile)
        xc = _ffn(xc.reshape(1, b * lx, d), mods, ng, wg, wu, wd, l, 0, cx_row,
                  tile_cf, tile).reshape(b, lx, d)
        (pa, ab, qr, qp, kr, vb, us, ut, cg, dq, dk, dv) = _inproj(
            x, mods, ng, w_in_b, qkg, mhead, cos, sin, *conv_par, l, lat, tile_big, tile)
        (pa_c, ab_c, _, qp_c, k_c, vb_c, us_c, ut_c, cg_c, dq_c, dk_c, dv_c) = _inproj(
            xc, mods, ng, w_in_b, qkg, mhead, cos_x, sin_x, *conv_par, l, cx_row, tile_x, tile_x)

        hf_c, hb_c = _scan(us_c, ut_c, jnp.zeros((b, 2, mw), F32), *lru_par, l, tile_x)
        h0 = jnp.stack([hf_c[:, lx - 1], hb_c[:, 0]], axis=1)
        hf, hb = _scan(us, ut, h0, *lru_par, l, tile_big)
        yb = _window_attention(qr, qp, kr, vb, k_c, vb_c, sink_rows, l, tile_win)
        yd = _nbr_attention(dq, dk, dv, dk_c, dv_c, slabs, slab_index, l, tile_nbr, nbr_sub)
        x = _merge(x, mods, ng, pa, ab, conv_a_w, yb, hf, hb, cg, yd, w_in_b, bg, wbr, wo, l, lat, tile_big, tile)
        x = _ffn(x, mods, ng, wg, wu, wd, l, 1, lat, tile_big, tile)
        if ctx_out:
            yb_c = jnp.swapaxes(_ctx_attention(qp_c, k_c, vb_c, sink, l, True, WIN_Q_COLS), 1, 2)
            yd_c = jnp.swapaxes(_ctx_attention(dq_c, dk_c, dv_c, sink, l, False, (0, 1, 2, 3)), 1, 2)
            xc = _merge(xc, mods, ng, pa_c, ab_c, conv_a_w, yb_c, hf_c, hb_c, cg_c, yd_c,
                        w_in_b, bg, wbr, wo, l, cx_row, tile_x, tile_x)
            xc = _ffn(xc.reshape(1, b * lx, d), mods, ng, wg, wu, wd, l, 1, cx_row,
                      tile_cf, tile).reshape(b, lx, d)
    return x
```
